```python
import math
import jax
import jax.numpy as jnp
from jax import lax
import numpy as np

D_MODEL = 1024
BATCH = 8
SEQ = 4096
DEPTH = 4

HEAD_DIM = 64
MOBA_HEADS = 8
SB_HEADS = 8
MOBA_WIDTH = MOBA_HEADS * HEAD_DIM
SB_WIDTH = SB_HEADS * HEAD_DIM
ATTN_IN = 3 * (MOBA_WIDTH + SB_WIDTH)
ATTN_OUT = MOBA_WIDTH + SB_WIDTH
MOBA_BLOCK = 256
MOBA_TOPK = 3
MOBA_Q_CHUNK = 32
SB_Q_BLOCK = 128
ROPE_THETA = 500000.0
ROPE_DIMS = HEAD_DIM // 4
GDN_HEADS = 8
GDN_HEAD_DIM = 128
GDN_WIDTH = GDN_HEADS * GDN_HEAD_DIM
GDN_IN = 4 * GDN_WIDTH + 2 * GDN_HEADS
GDN_CONV = 4
GDN_CHUNK = 64
FFN_HIDDEN = -(-8 * D_MODEL // (3 * 256)) * 256
N_EVEN = (DEPTH + 1) // 2
N_ODD = DEPTH // 2
EPS = 1e-6

kernel_name = 'hybrid_moba_stickbreak_gdn_adaln'


def rmsnorm(x, g):
    xf = x.astype(jnp.float32)
    y = xf * lax.rsqrt(jnp.mean(xf * xf, axis=-1, keepdims=True) + EPS)
    return (y * g.astype(jnp.float32)).astype(x.dtype)


def l2norm(x):
    xf = x.astype(jnp.float32)
    return xf * lax.rsqrt(jnp.sum(xf * xf, axis=-1, keepdims=True) + EPS)


def partial_rope(x, positions):
    half = ROPE_DIMS // 2
    inv_freq = ROPE_THETA ** (-jnp.arange(half, dtype=jnp.float32) * 2.0 / ROPE_DIMS)
    ang = positions.astype(jnp.float32)[:, :, None] * inv_freq
    cos = jnp.cos(ang)[:, :, None, :].astype(x.dtype)
    sin = jnp.sin(ang)[:, :, None, :].astype(x.dtype)
    x1 = x[..., :half]
    x2 = x[..., half:ROPE_DIMS]
    return jnp.concatenate([x1 * cos - x2 * sin, x2 * cos + x1 * sin, x[..., ROPE_DIMS:]], axis=-1)


def moba_attention(q, k, v):
    B, H, T, dh = q.shape
    scale = dh ** -0.5
    n_blk = -(-T // MOBA_BLOCK)
    pad = n_blk * MOBA_BLOCK - T
    kp = jnp.pad(k, ((0, 0), (0, 0), (0, pad), (0, 0)))
    vp = jnp.pad(v, ((0, 0), (0, 0), (0, pad), (0, 0)))
    k_blocks = kp.reshape(B, H, n_blk, MOBA_BLOCK, dh)
    v_blocks = vp.reshape(B, H, n_blk, MOBA_BLOCK, dh)
    k_mean = jnp.mean(k_blocks.astype(jnp.float32), axis=3)
    q_blk = jnp.arange(T) // MOBA_BLOCK
    past = jnp.arange(n_blk)[None, :] < q_blk[:, None]
    gate = jnp.einsum('bhtd,bhnd->bhtn', q.astype(jnp.float32), k_mean)
    gate = jnp.where(past, gate, -jnp.inf)
    n_sel = min(MOBA_TOPK, max(n_blk - 1, 1))
    _, sel = lax.top_k(gate, n_sel)
    sel_valid = sel < q_blk[:, None]

    n_qc = T // MOBA_Q_CHUNK
    def to_chunks(t):
        return jnp.moveaxis(t.reshape((B, H, n_qc, MOBA_Q_CHUNK) + t.shape[3:]), 2, 0)
    b_idx = jnp.arange(B)[:, None, None, None]
    h_idx = jnp.arange(H)[None, :, None, None]
    q_offs = jnp.arange(MOBA_Q_CHUNK)
    k_offs = jnp.arange(MOBA_BLOCK)

    def chunk(args):
        ci, q_c, sel_c, valid_c = args
        t0 = ci * MOBA_Q_CHUNK
        k_sel = k_blocks[b_idx, h_idx, sel_c]
        v_sel = v_blocks[b_idx, h_idx, sel_c]
        s_sel = jnp.einsum('bhqd,bhqnkd->bhqnk', q_c, k_sel).astype(jnp.float32) * scale
        s_sel = jnp.where(valid_c[..., None], s_sel, -jnp.inf)
        s_sel = s_sel.reshape(B, H, MOBA_Q_CHUNK, n_sel * MOBA_BLOCK)
        blk_start = (t0 // MOBA_BLOCK) * MOBA_BLOCK
        k_own = lax.dynamic_slice_in_dim(kp, blk_start, MOBA_BLOCK, axis=2)
        v_own = lax.dynamic_slice_in_dim(vp, blk_start, MOBA_BLOCK, axis=2)
        s_own = jnp.einsum('bhqd,bhkd->bhqk', q_c, k_own).astype(jnp.float32) * scale
        causal = (blk_start + k_offs)[None, :] <= (t0 + q_offs)[:, None]
        s_own = jnp.where(causal, s_own, -jnp.inf)
        p = jax.nn.softmax(jnp.concatenate([s_sel, s_own], axis=-1), axis=-1).astype(v.dtype)
        p_sel = p[..., :n_sel * MOBA_BLOCK].reshape(B, H, MOBA_Q_CHUNK, n_sel, MOBA_BLOCK)
        p_own = p[..., n_sel * MOBA_BLOCK:]
        return (jnp.einsum('bhqnk,bhqnkd->bhqd', p_sel, v_sel)
                + jnp.einsum('bhqk,bhkd->bhqd', p_own, v_own))

    out = lax.map(chunk, (jnp.arange(n_qc), to_chunks(q), to_chunks(sel), to_chunks(sel_valid)))
    return jnp.moveaxis(out, 0, 2).reshape(B, H, T, dh)


def stick_breaking_attention(q, k, v):
    B, H, T, dh = q.shape
    scale = dh ** -0.5
    outs = []
    for t0 in range(0, T, SB_Q_BLOCK):
        t1 = t0 + SB_Q_BLOCK
        z = jnp.einsum('bhqd,bhkd->bhqk', q[:, :, t0:t1], k[:, :, :t1]).astype(jnp.float32) * scale
        causal = jnp.arange(t1)[None, :] < jnp.arange(t0, t1)[:, None]
        log_1m = jnp.where(causal, jax.nn.log_sigmoid(-z), 0.0)
        log_after = lax.cumsum(log_1m, axis=3, reverse=True) - log_1m
        w = jnp.where(causal, jnp.exp(jax.nn.log_sigmoid(z) + log_after), 0.0)
        outs.append(jnp.einsum('bhqk,bhkd->bhqd', w.astype(v.dtype), v[:, :, :t1]))
    return jnp.concatenate(outs, axis=2)


def attention_head_groups(h, positions, w_in, w_out):
    B, T, _ = h.shape
    proj = h @ w_in
    cuts = [MOBA_WIDTH, 2 * MOBA_WIDTH, 3 * MOBA_WIDTH,
            3 * MOBA_WIDTH + SB_WIDTH, 3 * MOBA_WIDTH + 2 * SB_WIDTH]
    qa, ka, va, qb, kb, vb = (t.reshape(B, T, -1, HEAD_DIM) for t in jnp.split(proj, cuts, axis=-1))
    qa = partial_rope(qa, positions)
    ka = partial_rope(ka, positions)
    def bhtd(t):
        return t.transpose(0, 2, 1, 3)
    oa = moba_attention(bhtd(qa), bhtd(ka), bhtd(va))
    ob = stick_breaking_attention(bhtd(qb), bhtd(kb), bhtd(vb))
    o = jnp.concatenate([oa, ob], axis=1).transpose(0, 2, 1, 3).reshape(B, T, ATTN_OUT)
    return o @ w_out


def causal_depthwise_conv(x, w):
    ch = x.shape[-1]
    return lax.conv_general_dilated(
        x, w[:, None, :].astype(x.dtype), window_strides=(1,), padding=[(GDN_CONV - 1, 0)],
        dimension_numbers=('NWC', 'WIO', 'NWC'), feature_group_count=ch)


def chunk_gated_delta_rule(q, k, v, g, beta):
    B, T, H, dk = q.shape
    dv = v.shape[-1]
    C = GDN_CHUNK
    N = T // C
    def chunks(t):
        t = t.astype(jnp.float32).reshape((B, N, C, H) + t.shape[3:])
        return t.transpose((0, 3, 1, 2) + tuple(range(4, t.ndim)))
    q, k, v, g, beta = (chunks(t) for t in (q, k, v, g, beta))
    g = jnp.cumsum(g, axis=-1)
    idx = jnp.arange(C)
    incl = idx[:, None] >= idx[None, :]
    strict = idx[:, None] > idx[None, :]
    decay = jnp.exp(jnp.where(incl, g[..., :, None] - g[..., None, :], -jnp.inf))
    k_beta = k * beta[..., None]
    lower = jnp.where(strict, jnp.einsum('bhncd,bhnsd->bhncs', k_beta, k) * decay, 0.0)
    eye = jnp.eye(C, dtype=jnp.float32)
    rhs = jnp.concatenate([v * beta[..., None], k_beta * jnp.exp(g)[..., None]], axis=-1)
    sol = lax.linalg.triangular_solve(lower + eye, rhs, left_side=True, lower=True, unit_diagonal=True)
    u, w = sol[..., :dv], sol[..., dv:]
    attn = jnp.einsum('bhncd,bhnsd->bhncs', q, k) * decay

    def step(state, xs):
        q_c, k_c, u_c, w_c, g_c, a_c = xs
        v_new = u_c - jnp.einsum('bhcd,bhde->bhce', w_c, state)
        o = (jnp.einsum('bhcd,bhde->bhce', q_c * jnp.exp(g_c)[..., None], state)
             + jnp.einsum('bhcs,bhse->bhce', a_c, v_new))
        g_last = g_c[..., -1]
        k_dec = k_c * jnp.exp(g_last[..., None] - g_c)[..., None]
        state = state * jnp.exp(g_last)[..., None, None] + jnp.einsum('bhcd,bhce->bhde', k_dec, v_new)
        return state, o

    xs = tuple(jnp.moveaxis(t, 2, 0) for t in (q, k, u, w, g, attn))
    state0 = jnp.zeros((B, H, dk, dv), jnp.float32)
    _, o = lax.scan(step, state0, xs)
    return o.transpose(1, 0, 3, 2, 4).reshape(B, T, H, dv)


def gated_deltanet(h, w_in, conv_w, a_log, dt_bias, norm_g, w_out):
    B, T, _ = h.shape
    proj = h @ w_in
    qkv = jax.nn.silu(causal_depthwise_conv(proj[..., :3 * GDN_WIDTH], conv_w))
    z = proj[..., 3 * GDN_WIDTH:4 * GDN_WIDTH].reshape(B, T, GDN_HEADS, GDN_HEAD_DIM)
    a = proj[..., 4 * GDN_WIDTH:4 * GDN_WIDTH + GDN_HEADS].astype(jnp.float32)
    b = proj[..., 4 * GDN_WIDTH + GDN_HEADS:].astype(jnp.float32)
    q, k, v = (t.reshape(B, T, GDN_HEADS, GDN_HEAD_DIM) for t in jnp.split(qkv, 3, axis=-1))
    q = l2norm(q) * GDN_HEAD_DIM ** -0.5
    k = l2norm(k)
    beta = jax.nn.sigmoid(b)
    g = -jnp.exp(a_log.astype(jnp.float32)) * jax.nn.softplus(a + dt_bias.astype(jnp.float32))
    o = chunk_gated_delta_rule(q, k, v, g, beta)
    o = rmsnorm(o, norm_g) * jax.nn.silu(z.astype(jnp.float32))
    return o.reshape(B, T, GDN_WIDTH).astype(h.dtype) @ w_out


def swiglu(h, w_in, w_out):
    gate, up = jnp.split(h @ w_in, 2, axis=-1)
    return (jax.nn.silu(gate) * up) @ w_out


def setup_inputs(seed: int = 0) -> dict:
    key = jax.random.key(seed)
    ks = jax.random.split(key, 18)
    f32 = jnp.float32
    def dense(k, shape, fan_in, s=1.0):
        return jax.random.normal(k, shape, f32) * (s * fan_in ** -0.5)
    def gain(k, shape):
        return 1.0 + 0.05 * jax.random.normal(k, shape, f32)
    x = jax.random.normal(ks[0], (BATCH, SEQ, D_MODEL), f32)
    c = jax.random.normal(ks[1], (BATCH, D_MODEL), f32)
    offset = jax.random.randint(ks[2], (BATCH, 1), 0, 1024, dtype=jnp.int32)
    positions = offset + jnp.arange(SEQ, dtype=jnp.int32)[None, :]
    ada_w = dense(ks[3], (DEPTH, D_MODEL, 6 * D_MODEL), D_MODEL, 0.5)
    ada_b = 0.02 * jax.random.normal(ks[4], (DEPTH, 6 * D_MODEL), f32)
    norm_mix_g = gain(ks[5], (DEPTH, D_MODEL))
    norm_ffn_g = gain(ks[6], (DEPTH, D_MODEL))
    attn_w_in = dense(ks[7], (N_EVEN, D_MODEL, ATTN_IN), D_MODEL)
    attn_w_out = dense(ks[8], (N_EVEN, ATTN_OUT, D_MODEL), ATTN_OUT)
    gdn_w_in = dense(ks[9], (N_ODD, D_MODEL, GDN_IN), D_MODEL)
    gdn_conv_w = dense(ks[10], (N_ODD, GDN_CONV, 3 * GDN_WIDTH), GDN_CONV)
    gdn_a_log = jnp.log(jax.random.uniform(ks[11], (N_ODD, GDN_HEADS), f32, 1.0, 16.0))
    dt = jnp.exp(jax.random.uniform(ks[12], (N_ODD, GDN_HEADS), f32, math.log(1e-3), math.log(1e-1)))
    gdn_dt_bias = dt + jnp.log(-jnp.expm1(-dt))
    gdn_norm_g = gain(ks[13], (N_ODD, GDN_HEAD_DIM))
    gdn_w_out = dense(ks[14], (N_ODD, GDN_WIDTH, D_MODEL), GDN_WIDTH)
    ffn_w_in = dense(ks[15], (DEPTH, D_MODEL, 2 * FFN_HIDDEN), D_MODEL)
    ffn_w_out = dense(ks[16], (DEPTH, FFN_HIDDEN, D_MODEL), FFN_HIDDEN)
    final_norm_g = gain(ks[17], (D_MODEL,))
    return {'x': x, 'c': c, 'positions': positions, 'ada_w': ada_w, 'ada_b': ada_b,
            'norm_mix_g': norm_mix_g, 'norm_ffn_g': norm_ffn_g,
            'attn_w_in': attn_w_in, 'attn_w_out': attn_w_out,
            'gdn_w_in': gdn_w_in, 'gdn_conv_w': gdn_conv_w, 'gdn_a_log': gdn_a_log,
            'gdn_dt_bias': gdn_dt_bias, 'gdn_norm_g': gdn_norm_g, 'gdn_w_out': gdn_w_out,
            'ffn_w_in': ffn_w_in, 'ffn_w_out': ffn_w_out, 'final_norm_g': final_norm_g}


def reference(x, c, positions, ada_w, ada_b, norm_mix_g, norm_ffn_g, attn_w_in, attn_w_out,
              gdn_w_in, gdn_conv_w, gdn_a_log, gdn_dt_bias, gdn_norm_g, gdn_w_out,
              ffn_w_in, ffn_w_out, final_norm_g):
    cond = jax.nn.silu(c)
    for layer in range(DEPTH):
        mod = (cond @ ada_w[layer] + ada_b[layer])[:, None, :]
        shift_m, scale_m, gate_m, shift_f, scale_f, gate_f = jnp.split(mod, 6, axis=-1)
        h = rmsnorm(x, norm_mix_g[layer]) * (1.0 + scale_m) + shift_m
        i = layer // 2
        if layer % 2 == 0:
            mixed = attention_head_groups(h, positions, attn_w_in[i], attn_w_out[i])
        else:
            mixed = gated_deltanet(h, gdn_w_in[i], gdn_conv_w[i], gdn_a_log[i], gdn_dt_bias[i],
                                   gdn_norm_g[i], gdn_w_out[i])
        x = x + gate_m * mixed
        h = rmsnorm(x, norm_ffn_g[layer]) * (1.0 + scale_f) + shift_f
        x = x + gate_f * swiglu(h, ffn_w_in[layer], ffn_w_out[layer])
    return rmsnorm(x, final_norm_g)
```

```python
import functools

import jax
import jax.numpy as jnp
from jax import lax
from jax.experimental import pallas as pl
from jax.experimental.pallas import tpu as pltpu

F32 = jnp.float32
BF16 = jnp.bfloat16
HIGHEST = lax.Precision.HIGHEST

EPS = 1e-6
HEAD_DIM = 64
MOBA_HEADS = 8
SB_HEADS = 8
MOBA_BLOCK = 256
MOBA_TOPK = 3
ROPE_THETA = 500000.0
ROPE_DIMS = HEAD_DIM // 4
GDN_HEADS = 8
GDN_HEAD_DIM = 128
GDN_CONV = 4
GDN_CHUNK = 64
ATTN_SCALE = HEAD_DIM ** -0.5

LANES = 128
HALO_ROWS = 8
V7X_VMEM_LIMIT = 56 * 1024 * 1024
ROW_TILE = 512
MOD_COL_TILE = 1536
GDN_ROW_TILE = 256

SHIFT_M, SCALE_M, GATE_M, SHIFT_F, SCALE_F, GATE_F = range(6)


def _params(*sem):
    return pltpu.CompilerParams(dimension_semantics=sem, vmem_limit_bytes=V7X_VMEM_LIMIT)


def _silu(x):
    return x * jax.nn.sigmoid(x)


def _softplus(x):
    return jnp.maximum(x, 0.0) + jnp.log1p(jnp.exp(-jnp.abs(x)))


def _col_chunks(total, width):
    chunks, c0 = [], 0
    while c0 < total:
        cw = min(width, total - c0)
        chunks.append((c0, cw))
        c0 += cw
    return tuple(chunks)


def _mod_kernel(c_ref, w_ref, b_ref, o_ref):
    cond = _silu(c_ref[...])
    o_ref[0] = jnp.dot(cond, w_ref[0], precision=HIGHEST, preferred_element_type=F32) + b_ref[0]


def _modulation(c, ada_w, ada_b):
    depth, d, six_d = ada_w.shape
    b = c.shape[0]
    tn = MOD_COL_TILE
    return pl.pallas_call(
        _mod_kernel,
        out_shape=jax.ShapeDtypeStruct((depth, b, six_d), F32),
        grid=(depth, six_d // tn),
        in_specs=[
            pl.BlockSpec((b, d), lambda l, j: (0, 0)),
            pl.BlockSpec((1, d, tn), lambda l, j: (l, 0, j)),
            pl.BlockSpec((1, 1, tn), lambda l, j: (l, 0, j)),
        ],
        out_specs=pl.BlockSpec((1, b, tn), lambda l, j: (l, 0, j)),
        compiler_params=_params("arbitrary", "arbitrary"),
        name="adaln_mod",
    )(c, ada_w, ada_b.reshape(depth, 1, six_d))


def _rope_kernel(pos_ref, freq_ref, ma_ref, mb_ref, c_ref, sa_ref, sb_ref):
    ang = pos_ref[...] * freq_ref[...]
    s = jnp.sin(ang)
    c_ref[...] = jnp.cos(ang)
    sa_ref[...] = s * ma_ref[...]
    sb_ref[...] = s * mb_ref[...]


def _rope_tables(positions):
    n = positions.size
    half = ROPE_DIMS // 2
    inv_freq = ROPE_THETA ** (-jnp.arange(half, dtype=F32) * 2.0 / ROPE_DIMS)
    lane = jnp.arange(LANES) % HEAD_DIM
    freq = jnp.where(lane < ROPE_DIMS, inv_freq[lane % half], 0.0).astype(F32)[None, :]
    ma = jnp.where(lane < half, -1.0, 0.0).astype(F32)[None, :]
    mb = jnp.where((lane >= half) & (lane < ROPE_DIMS), 1.0, 0.0).astype(F32)[None, :]
    pos = positions.astype(F32).reshape(n, 1)
    tm = min(ROW_TILE, n)
    row = pl.BlockSpec((1, LANES), lambda i: (0, 0))
    tab = pl.BlockSpec((tm, LANES), lambda i: (i, 0))
    return pl.pallas_call(
        _rope_kernel,
        out_shape=[jax.ShapeDtypeStruct((n, LANES), F32)] * 3,
        grid=(n // tm,),
        in_specs=[pl.BlockSpec((tm, 1), lambda i: (i, 0)), row, row, row],
        out_specs=[tab, tab, tab],
        compiler_params=_params("arbitrary"),
        name="rope_tables",
    )(pos, freq, ma, mb)


def _norm_mod(x, g_row, mod_ref, shift_idx, scale_idx):
    y = x * lax.rsqrt(jnp.mean(x * x, axis=-1, keepdims=True) + EPS) * g_row
    return y * (1.0 + mod_ref[0, scale_idx:scale_idx + 1, :]) + mod_ref[0, shift_idx:shift_idx + 1, :]


def _nm_kernel(*refs, chunks, rope_cols):
    if rope_cols:
        x_ref, g_ref, mod_ref, w_ref, c_ref, sa_ref, sb_ref, o_ref, h_scr = refs
    else:
        x_ref, g_ref, mod_ref, w_ref, o_ref, h_scr = refs
    h = _norm_mod(x_ref[...], g_ref[...], mod_ref, SHIFT_M, SCALE_M)
    h_scr[...] = h.astype(BF16)
    for c0, cw in chunks:
        acc = jnp.dot(h_scr[...], w_ref[:, c0:c0 + cw], preferred_element_type=F32)
        if c0 < rope_cols:
            reps = cw // LANES
            cos = jnp.tile(c_ref[...], (1, reps))
            sa = jnp.tile(sa_ref[...], (1, reps))
            sb = jnp.tile(sb_ref[...], (1, reps))
            half = ROPE_DIMS // 2
            acc = (acc * cos + pltpu.roll(acc, cw - half, axis=1) * sa
                   + pltpu.roll(acc, half, axis=1) * sb)
        o_ref[:, c0:c0 + cw] = acc.astype(o_ref.dtype)


def _norm_mod_proj(x2, g, mod, w, t, *, rope=None, rope_cols=0, chunk=512):
    n, d = x2.shape
    ncol = w.shape[1]
    tm = min(ROW_TILE, t)
    tpb = t // tm
    chunks = _col_chunks(ncol, chunk)
    assert all(cw % LANES == 0 for _, cw in chunks)
    assert all(c0 + cw <= rope_cols or c0 >= rope_cols for c0, cw in chunks)
    in_specs = [
        pl.BlockSpec((tm, d), lambda i: (i, 0)),
        pl.BlockSpec((1, d), lambda i: (0, 0)),
        pl.BlockSpec((1, 6, d), lambda i: (i // tpb, 0, 0)),
        pl.BlockSpec((d, ncol), lambda i: (0, 0), pipeline_mode=pl.Buffered(1)),
    ]
    args = [x2, g.reshape(1, d), mod, w]
    if rope_cols:
        tab = pl.BlockSpec((tm, LANES), lambda i: (i, 0))
        in_specs += [tab, tab, tab]
        args += list(rope)
    return pl.pallas_call(
        functools.partial(_nm_kernel, chunks=chunks, rope_cols=rope_cols),
        out_shape=jax.ShapeDtypeStruct((n, ncol), F32),
        grid=(n // tm,),
        in_specs=in_specs,
        out_specs=pl.BlockSpec((tm, ncol), lambda i: (i, 0)),
        scratch_shapes=[pltpu.VMEM((tm, d), BF16)],
        compiler_params=_params("arbitrary"),
        name="norm_mod_proj",
    )(*args)


def _moba_kernel(qt_ref, k_ref, vt_ref, o_ref, kb_scr, kmean_scr, sel_scr, *, nblk, n_sel):
    i = pl.program_id(2)

    @pl.when(i == 0)
    def _():
        kmean_scr[...] = jnp.zeros_like(kmean_scr)
        for j in range(nblk):
            kj = k_ref[0, 0, j]
            kmean_scr[j:j + 1, :] = jnp.mean(kj, axis=0, keepdims=True)
            kb_scr[j] = kj.astype(BF16)

    qt = qt_ref[0, 0]
    gate = jnp.dot(kmean_scr[...], qt, precision=HIGHEST, preferred_element_type=F32)
    row = lax.broadcasted_iota(jnp.int32, gate.shape, 0)
    cnt = jnp.zeros(gate.shape, jnp.int32)
    for jp in range(nblk):
        gj = gate[jp:jp + 1, :]
        beats = (gj > gate) | ((gj == gate) & (jp < row))
        cnt = cnt + jnp.where(beats, 1, 0) * (jp < i).astype(jnp.int32)
    sel = (row < i) & (cnt < n_sel)
    sel_scr[...] = jnp.where(sel, 1.0, 0.0)

    qs = (qt * ATTN_SCALE).astype(BF16)
    s = jnp.dot(kb_scr[i], qs, preferred_element_type=F32)
    kidx = lax.broadcasted_iota(jnp.int32, s.shape, 0)
    qidx = lax.broadcasted_iota(jnp.int32, s.shape, 1)
    s = jnp.where(kidx <= qidx, s, -jnp.inf)
    m = jnp.max(s, axis=0, keepdims=True)
    p = jnp.exp(s - m)
    l = jnp.sum(p, axis=0, keepdims=True)
    acc = jnp.dot(vt_ref[0, 0, i], p.astype(BF16), preferred_element_type=F32)

    def body(j, carry):
        m, l, acc = carry
        s = jnp.dot(kb_scr[j], qs, preferred_element_type=F32)
        s = jnp.where(sel_scr[pl.ds(j, 1), :] > 0.5, s, -jnp.inf)
        m_new = jnp.maximum(m, jnp.max(s, axis=0, keepdims=True))
        alpha = jnp.exp(m - m_new)
        p = jnp.exp(s - m_new)
        l = l * alpha + jnp.sum(p, axis=0, keepdims=True)
        acc = acc * alpha + jnp.dot(vt_ref[0, 0, j], p.astype(BF16), preferred_element_type=F32)
        return m_new, l, acc

    m, l, acc = lax.fori_loop(0, i, body, (m, l, acc))
    o_ref[0, 0] = acc / l


def _moba_attention(qt, kblk, vtblk):
    b, h, dh, t = qt.shape
    nblk, blk = kblk.shape[2], kblk.shape[3]
    n_sel = min(MOBA_TOPK, max(nblk - 1, 1))
    gate_rows = -(-nblk // 8) * 8
    return pl.pallas_call(
        functools.partial(_moba_kernel, nblk=nblk, n_sel=n_sel),
        out_shape=jax.ShapeDtypeStruct((b, h, dh, t), F32),
        grid=(b, h, nblk),
        in_specs=[
            pl.BlockSpec((1, 1, dh, blk), lambda bi, hi, i: (bi, hi, 0, i)),
            pl.BlockSpec((1, 1, nblk, blk, dh), lambda bi, hi, i: (bi, hi, 0, 0, 0)),
            pl.BlockSpec((1, 1, nblk, dh, blk), lambda bi, hi, i: (bi, hi, 0, 0, 0)),
        ],
        out_specs=pl.BlockSpec((1, 1, dh, blk), lambda bi, hi, i: (bi, hi, 0, i)),
        scratch_shapes=[
            pltpu.VMEM((nblk, blk, dh), BF16),
            pltpu.VMEM((gate_rows, dh), F32),
            pltpu.VMEM((gate_rows, blk), F32),
        ],
        compiler_params=_params("arbitrary", "arbitrary", "arbitrary"),
        name="moba_attention",
    )(qt, kblk, vtblk)


def _sb_kernel(qt_ref, k_ref, vt_ref, o_ref):
    i = pl.program_id(2)
    qs = (qt_ref[0, 0].astype(F32) * ATTN_SCALE).astype(BF16)
    blk, tq = k_ref.shape[3], qs.shape[1]
    kidx = lax.broadcasted_iota(jnp.int32, (blk, tq), 0)
    qidx = lax.broadcasted_iota(jnp.int32, (blk, tq), 1)
    causal = kidx < qidx
    r = lax.broadcasted_iota(jnp.int32, (blk, blk), 0)
    c = lax.broadcasted_iota(jnp.int32, (blk, blk), 1)
    later = jnp.where(c > r, 1.0, 0.0).astype(BF16)

    def block(j, carry, acc, diag):
        z = jnp.dot(k_ref[0, 0, j], qs, preferred_element_type=F32)
        sp = _softplus(z)
        l1m = -sp
        if diag:
            l1m = jnp.where(causal, l1m, 0.0)
        hi = l1m.astype(BF16)
        lo = (l1m - hi.astype(F32)).astype(BF16)
        la = (jnp.dot(later, hi, preferred_element_type=F32)
              + jnp.dot(later, lo, preferred_element_type=F32) + carry)
        w = jnp.exp(z - sp + la)
        if diag:
            w = jnp.where(causal, w, 0.0)
        acc = acc + jnp.dot(vt_ref[0, 0, j], w.astype(BF16), preferred_element_type=F32)
        return la[0:1, :] + l1m[0:1, :], acc

    carry0 = jnp.zeros((1, tq), F32)
    acc0 = jnp.zeros((qs.shape[0], tq), F32)
    carry, acc = block(i, carry0, acc0, True)

    def body(step, ca):
        return block(i - 1 - step, ca[0], ca[1], False)

    carry, acc = lax.fori_loop(0, i, body, (carry, acc))
    o_ref[0, 0] = acc


def _sb_attention(qt, kblk, vtblk):
    b, h, dh, t = qt.shape
    nblk, blk = kblk.shape[2], kblk.shape[3]
    return pl.pallas_call(
        _sb_kernel,
        out_shape=jax.ShapeDtypeStruct((b, h, dh, t), F32),
        grid=(b, h, nblk),
        in_specs=[
            pl.BlockSpec((1, 1, dh, blk), lambda bi, hi, i: (bi, hi, 0, i)),
            pl.BlockSpec((1, 1, nblk, blk, dh), lambda bi, hi, i: (bi, hi, 0, 0, 0)),
            pl.BlockSpec((1, 1, nblk, dh, blk), lambda bi, hi, i: (bi, hi, 0, 0, 0)),
        ],
        out_specs=pl.BlockSpec((1, 1, dh, blk), lambda bi, hi, i: (bi, hi, 0, i)),
        compiler_params=_params("arbitrary", "arbitrary", "arbitrary"),
        name="sb_attention",
    )(qt, kblk, vtblk)


def _proj_res_kernel(o_ref, x_ref, mod_ref, w_ref, out_ref):
    y = jnp.dot(o_ref[...], w_ref[...], preferred_element_type=F32)
    out_ref[...] = x_ref[...] + mod_ref[0, GATE_M:GATE_M + 1, :] * y


def _proj_residual(o2, x2, mod, w, t):
    n, d = x2.shape
    kdim = o2.shape[1]
    tm = min(ROW_TILE, t)
    tpb = t // tm
    return pl.pallas_call(
        _proj_res_kernel,
        out_shape=jax.ShapeDtypeStruct((n, d), F32),
        grid=(n // tm,),
        in_specs=[
            pl.BlockSpec((tm, kdim), lambda i: (i, 0)),
            pl.BlockSpec((tm, d), lambda i: (i, 0)),
            pl.BlockSpec((1, 6, d), lambda i: (i // tpb, 0, 0)),
            pl.BlockSpec((kdim, d), lambda i: (0, 0), pipeline_mode=pl.Buffered(1)),
        ],
        out_specs=pl.BlockSpec((tm, d), lambda i: (i, 0)),
        compiler_params=_params("arbitrary"),
        name="proj_residual",
    )(o2, x2, mod, w)


def _ffn_kernel(x_ref, g_ref, mod_ref, wg_ref, wu_ref, wo_ref, out_ref, h_scr, a_scr, *, chunks):
    x = x_ref[...]
    h_scr[...] = _norm_mod(x, g_ref[...], mod_ref, SHIFT_F, SCALE_F).astype(BF16)
    for c0, cw in chunks:
        gate = jnp.dot(h_scr[...], wg_ref[:, c0:c0 + cw], preferred_element_type=F32)
        up = jnp.dot(h_scr[...], wu_ref[:, c0:c0 + cw], preferred_element_type=F32)
        a_scr[:, c0:c0 + cw] = (_silu(gate) * up).astype(BF16)
    y = jnp.dot(a_scr[...], wo_ref[...], preferred_element_type=F32)
    out_ref[...] = x + mod_ref[0, GATE_F:GATE_F + 1, :] * y


def _ffn(x2, g, mod, wg, wu, wo, t):
    n, d = x2.shape
    f = wg.shape[1]
    tm = min(ROW_TILE, t)
    tpb = t // tm
    chunks = _col_chunks(f, 256)
    const = lambda shape: pl.BlockSpec(shape, lambda i: (0, 0), pipeline_mode=pl.Buffered(1))
    return pl.pallas_call(
        functools.partial(_ffn_kernel, chunks=chunks),
        out_shape=jax.ShapeDtypeStruct((n, d), F32),
        grid=(n // tm,),
        in_specs=[
            pl.BlockSpec((tm, d), lambda i: (i, 0)),
            pl.BlockSpec((1, d), lambda i: (0, 0)),
            pl.BlockSpec((1, 6, d), lambda i: (i // tpb, 0, 0)),
            const((d, f)), const((d, f)), const((f, d)),
        ],
        out_specs=pl.BlockSpec((tm, d), lambda i: (i, 0)),
        scratch_shapes=[pltpu.VMEM((tm, d), BF16), pltpu.VMEM((tm, f), BF16)],
        compiler_params=_params("arbitrary"),
        name="swiglu_ffn",
    )(x2, g.reshape(1, d), mod, wg, wu, wo)


def _gdn_kernel(q_ref, k_ref, v_ref, z_ref, ab_ref, wq_ref, wk_ref, wv_ref, alog_ref, dtb_ref,
                ng_ref, o_ref, s_scr, halo_scr, buf_scr, *, tt):
    hd = pl.program_id(1)
    it = pl.program_id(2)
    ck = GDN_CHUNK
    dk = GDN_HEAD_DIM

    @pl.when(it == 0)
    def _():
        s_scr[...] = jnp.zeros_like(s_scr)
        halo_scr[...] = jnp.zeros_like(halo_scr)

    def conv_silu(x_ref, w_ref, idx):
        x = x_ref[0]
        buf_scr[idx, 0:HALO_ROWS, :] = halo_scr[idx]
        buf_scr[idx, HALO_ROWS:HALO_ROWS + tt, :] = x
        halo_scr[idx] = x[tt - HALO_ROWS:tt, :]
        y = jnp.zeros_like(x)
        for j in range(GDN_CONV):
            y = y + w_ref[j:j + 1, :] * buf_scr[idx, pl.ds(HALO_ROWS - (GDN_CONV - 1) + j, tt), :]
        return _silu(y)

    def l2norm(x):
        return x * lax.rsqrt(jnp.sum(x * x, axis=-1, keepdims=True) + EPS)

    qn = l2norm(conv_silu(q_ref, wq_ref, 0)) * (dk ** -0.5)
    kn = l2norm(conv_silu(k_ref, wk_ref, 1))
    vv = conv_silu(v_ref, wv_ref, 2)

    lane_src = lax.broadcasted_iota(jnp.int32, (LANES, LANES), 0)
    pick_a = jnp.where(lane_src == hd, 1.0, 0.0)
    pick_b = jnp.where(lane_src == hd + GDN_HEADS, 1.0, 0.0)
    ab = ab_ref[0]
    a_b = jnp.dot(ab, pick_a, precision=HIGHEST, preferred_element_type=F32)
    b_b = jnp.dot(ab, pick_b, precision=HIGHEST, preferred_element_type=F32)
    alog = jnp.dot(jnp.broadcast_to(alog_ref[...], (8, LANES)), pick_a, precision=HIGHEST,
                   preferred_element_type=F32)[0:1, :]
    dtb = jnp.dot(jnp.broadcast_to(dtb_ref[...], (8, LANES)), pick_a, precision=HIGHEST,
                  preferred_element_type=F32)[0:1, :]
    g_b = -jnp.exp(alog) * _softplus(a_b + dtb)
    beta_b = jax.nn.sigmoid(b_b)

    ri = lax.broadcasted_iota(jnp.int32, (ck, ck), 0)
    ci = lax.broadcasted_iota(jnp.int32, (ck, ck), 1)
    lower_incl = jnp.where(ri >= ci, 1.0, 0.0)
    eye = jnp.where(ri == ci, 1.0, 0.0)
    ones = jnp.ones((ck, ck), F32)
    nt = (((1,), (1,)), ((), ()))
    tn = (((0,), (0,)), ((), ()))

    state = s_scr[...]
    for cidx in range(tt // ck):
        rows = slice(cidx * ck, (cidx + 1) * ck)
        gch = g_b[rows]
        gc = jnp.dot(lower_incl, gch, precision=HIGHEST, preferred_element_type=F32)
        gcol = gc[:, :ck]
        grow = jnp.dot(ones, jnp.where(ri <= ci, gch[:, :ck], 0.0), precision=HIGHEST,
                       preferred_element_type=F32)
        decay = jnp.where(ri >= ci, jnp.exp(gcol - grow), 0.0)
        qc, kc, vc = qn[rows], kn[rows], vv[rows]
        beta = beta_b[rows]
        kbeta = kc * beta
        kb16 = kc.astype(BF16)
        a_mat = jnp.where(ri > ci, lax.dot_general(kbeta.astype(BF16), kb16, nt,
                                                   preferred_element_type=F32) * decay, 0.0)
        attn = lax.dot_general(qc.astype(BF16), kb16, nt, preferred_element_type=F32) * decay
        inv = eye - a_mat
        pw = a_mat
        n_sq = (ck - 1).bit_length() - 1
        for _ in range(n_sq):
            pw = jnp.dot(pw, pw, precision=HIGHEST, preferred_element_type=F32)
            inv = inv + jnp.dot(inv, pw, precision=HIGHEST, preferred_element_type=F32)
        egc = jnp.exp(gc)
        rhs = jnp.concatenate([vc * beta, kbeta * egc], axis=1)
        sol = jnp.dot(inv, rhs, precision=HIGHEST, preferred_element_type=F32)
        u, w = sol[:, :dk], sol[:, dk:]

        s16 = state.astype(BF16)
        v_new = u - jnp.dot(w.astype(BF16), s16, preferred_element_type=F32)
        vn16 = v_new.astype(BF16)
        o_c = (jnp.dot((qc * egc).astype(BF16), s16, preferred_element_type=F32)
               + jnp.dot(attn.astype(BF16), vn16, preferred_element_type=F32))
        g_last = gc[ck - 1:ck, :]
        k_dec = kc * jnp.exp(g_last - gc)
        state = state * jnp.exp(g_last) + lax.dot_general(k_dec.astype(BF16), vn16, tn,
                                                          preferred_element_type=F32)
        on = o_c * lax.rsqrt(jnp.mean(o_c * o_c, axis=-1, keepdims=True) + EPS) * ng_ref[...]
        o_ref[0, rows, :] = (on * _silu(z_ref[0, rows, :])).astype(o_ref.dtype)
    s_scr[...] = state


def _gated_deltanet(proj3, conv_w, a_log, dt_bias, norm_g):
    b, t, _ = proj3.shape
    nh, dk = GDN_HEADS, GDN_HEAD_DIM
    tt = min(GDN_ROW_TILE, t)
    col = lambda off: pl.BlockSpec((1, tt, dk), lambda bi, hi, i: (bi, i, off + hi))
    wcol = lambda off: pl.BlockSpec((GDN_CONV, dk), lambda bi, hi, i: (0, off + hi))
    row = pl.BlockSpec((1, LANES), lambda bi, hi, i: (0, 0))
    pad = lambda v: jnp.pad(v.astype(F32), (0, LANES - v.shape[0])).reshape(1, LANES)
    return pl.pallas_call(
        functools.partial(_gdn_kernel, tt=tt),
        out_shape=jax.ShapeDtypeStruct((b, t, nh * dk), BF16),
        grid=(b, nh, t // tt),
        in_specs=[
            col(0), col(nh), col(2 * nh), col(3 * nh),
            pl.BlockSpec((1, tt, LANES), lambda bi, hi, i: (bi, i, 4 * nh)),
            wcol(0), wcol(nh), wcol(2 * nh),
            row, row, row,
        ],
        out_specs=pl.BlockSpec((1, tt, dk), lambda bi, hi, i: (bi, i, hi)),
        scratch_shapes=[
            pltpu.VMEM((dk, dk), F32),
            pltpu.VMEM((3, HALO_ROWS, dk), F32),
            pltpu.VMEM((3, HALO_ROWS + tt, dk), F32),
        ],
        compiler_params=_params("arbitrary", "arbitrary", "arbitrary"),
        name="gated_deltanet",
    )(proj3, proj3, proj3, proj3, proj3, conv_w, conv_w, conv_w,
      pad(a_log), pad(dt_bias), norm_g.astype(F32).reshape(1, dk))


def _final_norm_kernel(x_ref, g_ref, o_ref):
    x = x_ref[...]
    o_ref[...] = x * lax.rsqrt(jnp.mean(x * x, axis=-1, keepdims=True) + EPS) * g_ref[...]


def _final_norm(x2, g):
    n, d = x2.shape
    tm = min(ROW_TILE, n)
    return pl.pallas_call(
        _final_norm_kernel,
        out_shape=jax.ShapeDtypeStruct((n, d), F32),
        grid=(n // tm,),
        in_specs=[pl.BlockSpec((tm, d), lambda i: (i, 0)), pl.BlockSpec((1, d), lambda i: (0, 0))],
        out_specs=pl.BlockSpec((tm, d), lambda i: (i, 0)),
        compiler_params=_params("arbitrary"),
        name="final_norm",
    )(x2, g.reshape(1, d))


def _attention_layer(x2, g, mod, w_in, w_out, rope, b, t):
    mw = MOBA_HEADS * HEAD_DIM
    sw = SB_HEADS * HEAD_DIM
    nblk = t // MOBA_BLOCK
    proj = _norm_mod_proj(x2, g, mod, w_in.astype(BF16), t, rope=rope, rope_cols=2 * mw)

    def heads(c0, nh):
        return proj[:, c0:c0 + nh * HEAD_DIM].reshape(b, nblk, MOBA_BLOCK, nh, HEAD_DIM)

    def feature_major(c0, nh, dtype):
        return heads(c0, nh).transpose(0, 3, 4, 1, 2).reshape(b, nh, HEAD_DIM, t).astype(dtype)

    def key_blocks(c0, nh, dtype):
        return heads(c0, nh).transpose(0, 3, 1, 2, 4).astype(dtype)

    def value_blocks(c0, nh):
        return heads(c0, nh).transpose(0, 3, 1, 4, 2).astype(BF16)

    oa = _moba_attention(feature_major(0, MOBA_HEADS, F32), key_blocks(mw, MOBA_HEADS, F32),
                         value_blocks(2 * mw, MOBA_HEADS))
    s0 = 3 * mw
    ob = _sb_attention(feature_major(s0, SB_HEADS, BF16), key_blocks(s0 + sw, SB_HEADS, BF16),
                       value_blocks(s0 + 2 * sw, SB_HEADS))
    o = jnp.concatenate([oa, ob], axis=1)
    o2 = o.transpose(0, 3, 1, 2).reshape(b * t, mw + sw).astype(BF16)
    return _proj_residual(o2, x2, mod, w_out.astype(BF16), t)


def _gdn_layer(x2, g, mod, w_in, conv_w, a_log, dt_bias, norm_g, w_out, b, t):
    gw = GDN_HEADS * GDN_HEAD_DIM
    w_main = w_in[:, :4 * gw]
    w_ab = jnp.pad(w_in[:, 4 * gw:], ((0, 0), (0, LANES - 2 * GDN_HEADS)))
    w_cat = jnp.concatenate([w_main, w_ab], axis=1).astype(BF16)
    proj = _norm_mod_proj(x2, g, mod, w_cat, t, chunk=384)
    o = _gated_deltanet(proj.reshape(b, t, -1), conv_w, a_log, dt_bias, norm_g)
    return _proj_residual(o.reshape(b * t, gw), x2, mod, w_out.astype(BF16), t)


def kernel(x, c, positions, ada_w, ada_b, norm_mix_g, norm_ffn_g, attn_w_in, attn_w_out, gdn_w_in,
           gdn_conv_w, gdn_a_log, gdn_dt_bias, gdn_norm_g, gdn_w_out, ffn_w_in, ffn_w_out,
           final_norm_g):
    b, t, d = x.shape
    depth = ada_w.shape[0]
    assert t % MOBA_BLOCK == 0 and t % GDN_CHUNK == 0
    mods = _modulation(c, ada_w, ada_b)
    rope = _rope_tables(positions)
    x2 = x.reshape(b * t, d)
    f = ffn_w_out.shape[1]
    for layer in range(depth):
        mod = mods[layer].reshape(b, 6, d)
        i = layer // 2
        if layer % 2 == 0:
            x2 = _attention_layer(x2, norm_mix_g[layer], mod, attn_w_in[i], attn_w_out[i], rope, b, t)
        else:
            x2 = _gdn_layer(x2, norm_mix_g[layer], mod, gdn_w_in[i], gdn_conv_w[i], gdn_a_log[i],
                            gdn_dt_bias[i], gdn_norm_g[i], gdn_w_out[i], b, t)
        w_in = ffn_w_in[layer].astype(BF16)
        x2 = _ffn(x2, norm_ffn_g[layer], mod, w_in[:, :f], w_in[:, f:], ffn_w_out[layer].astype(BF16), t)
    return _final_norm(x2, final_norm_g).reshape(b, t, d)
```

```python
import functools

import jax
import jax.numpy as jnp
from jax import lax
from jax.experimental import pallas as pl
from jax.experimental.pallas import tpu as pltpu

F32 = jnp.float32
BF16 = jnp.bfloat16
HIGHEST = lax.Precision.HIGHEST

EPS = 1e-6
HEAD_DIM = 64
MOBA_HEADS = 8
SB_HEADS = 8
MOBA_BLOCK = 256
MOBA_TOPK = 3
ROPE_THETA = 500000.0
ROPE_DIMS = HEAD_DIM // 4
GDN_HEADS = 8
GDN_HEAD_DIM = 128
GDN_CONV = 4
GDN_CHUNK = 64
ATTN_SCALE = HEAD_DIM ** -0.5

LANES = 128
HALO_ROWS = 8
V7X_VMEM_LIMIT = 56 * 1024 * 1024
ROW_TILE = 512
MOD_COL_TILE = 1536
GDN_ROW_TILE = 128
ATTN_HEADS_PER_STEP = 4

SHIFT_M, SCALE_M, GATE_M, SHIFT_F, SCALE_F, GATE_F = range(6)


def _params(*sem):
    return pltpu.CompilerParams(dimension_semantics=sem, vmem_limit_bytes=V7X_VMEM_LIMIT)


def _silu(x):
    return x * jax.nn.sigmoid(x)


def _softplus(x):
    return jnp.maximum(x, 0.0) + jnp.log(1.0 + jnp.exp(-jnp.abs(x)))


def _dot(a, b):
    return jnp.dot(a, b, preferred_element_type=F32)


def _split2(x):
    hi = x.astype(BF16)
    return hi, (x - hi.astype(F32)).astype(BF16)


def _col_chunks(total, width):
    chunks, c0 = [], 0
    while c0 < total:
        cw = min(width, total - c0)
        chunks.append((c0, cw))
        c0 += cw
    return tuple(chunks)


def _mod_kernel(c_ref, w_ref, b_ref, o_ref):
    cond = _silu(c_ref[...])
    o_ref[0] = jnp.dot(cond, w_ref[0], precision=HIGHEST, preferred_element_type=F32) + b_ref[0]


def _modulation(c, ada_w, ada_b):
    depth, d, six_d = ada_w.shape
    b = c.shape[0]
    tn = MOD_COL_TILE
    return pl.pallas_call(
        _mod_kernel,
        out_shape=jax.ShapeDtypeStruct((depth, b, six_d), F32),
        grid=(depth, six_d // tn),
        in_specs=[
            pl.BlockSpec((b, d), lambda l, j: (0, 0)),
            pl.BlockSpec((1, d, tn), lambda l, j: (l, 0, j)),
            pl.BlockSpec((1, 1, tn), lambda l, j: (l, 0, j)),
        ],
        out_specs=pl.BlockSpec((1, b, tn), lambda l, j: (l, 0, j)),
        compiler_params=_params("arbitrary", "arbitrary"),
        name="adaln_mod",
    )(c, ada_w, ada_b.reshape(depth, 1, six_d))


def _rope_kernel(pos_ref, freq_ref, ma_ref, mb_ref, c_ref, sa_ref, sb_ref):
    ang = pos_ref[...] * freq_ref[...]
    s = jnp.sin(ang)
    c_ref[...] = jnp.cos(ang)
    sa_ref[...] = s * ma_ref[...]
    sb_ref[...] = s * mb_ref[...]


def _rope_tables(positions):
    n = positions.size
    half = ROPE_DIMS // 2
    inv_freq = ROPE_THETA ** (-jnp.arange(half, dtype=F32) * 2.0 / ROPE_DIMS)
    lane = jnp.arange(LANES) % HEAD_DIM
    freq = jnp.where(lane < ROPE_DIMS, inv_freq[lane % half], 0.0).astype(F32)[None, :]
    ma = jnp.where(lane < half, -1.0, 0.0).astype(F32)[None, :]
    mb = jnp.where((lane >= half) & (lane < ROPE_DIMS), 1.0, 0.0).astype(F32)[None, :]
    pos = positions.astype(F32).reshape(n, 1)
    tm = min(ROW_TILE, n)
    row = pl.BlockSpec((1, LANES), lambda i: (0, 0))
    tab = pl.BlockSpec((tm, LANES), lambda i: (i, 0))
    return pl.pallas_call(
        _rope_kernel,
        out_shape=[jax.ShapeDtypeStruct((n, LANES), F32)] * 3,
        grid=(n // tm,),
        in_specs=[pl.BlockSpec((tm, 1), lambda i: (i, 0)), row, row, row],
        out_specs=[tab, tab, tab],
        compiler_params=_params("arbitrary"),
        name="rope_tables",
    )(pos, freq, ma, mb)


def _norm_mod(x, g_row, mod_ref, shift_idx, scale_idx):
    y = x * lax.rsqrt(jnp.mean(x * x, axis=-1, keepdims=True) + EPS) * g_row
    return y * (1.0 + mod_ref[0, scale_idx:scale_idx + 1, :]) + mod_ref[0, shift_idx:shift_idx + 1, :]


def _nm_kernel(*refs, chunks, rope_cols):
    if rope_cols:
        x_ref, g_ref, mod_ref, w_ref, c_ref, sa_ref, sb_ref, o_ref, h_scr = refs
    else:
        x_ref, g_ref, mod_ref, w_ref, o_ref, h_scr = refs
    h = _norm_mod(x_ref[...], g_ref[...], mod_ref, SHIFT_M, SCALE_M)
    h_scr[...] = h.astype(BF16)
    for c0, cw in chunks:
        acc = _dot(h_scr[...], w_ref[:, c0:c0 + cw])
        if c0 < rope_cols:
            reps = cw // LANES
            cos = jnp.tile(c_ref[...], (1, reps))
            sa = jnp.tile(sa_ref[...], (1, reps))
            sb = jnp.tile(sb_ref[...], (1, reps))
            half = ROPE_DIMS // 2
            acc = (acc * cos + pltpu.roll(acc, cw - half, axis=1) * sa
                   + pltpu.roll(acc, half, axis=1) * sb)
        o_ref[:, c0:c0 + cw] = acc.astype(o_ref.dtype)


def _norm_mod_proj(x2, g, mod, w, t, *, rope=None, rope_cols=0, chunk=512):
    n, d = x2.shape
    ncol = w.shape[1]
    tm = min(ROW_TILE, t)
    tpb = t // tm
    chunks = _col_chunks(ncol, chunk)
    assert all(cw % LANES == 0 for _, cw in chunks)
    assert all(c0 + cw <= rope_cols or c0 >= rope_cols for c0, cw in chunks)
    in_specs = [
        pl.BlockSpec((tm, d), lambda i: (i, 0)),
        pl.BlockSpec((1, d), lambda i: (0, 0)),
        pl.BlockSpec((1, 6, d), lambda i: (i // tpb, 0, 0)),
        pl.BlockSpec((d, ncol), lambda i: (0, 0), pipeline_mode=pl.Buffered(1)),
    ]
    args = [x2, g.reshape(1, d), mod, w]
    if rope_cols:
        tab = pl.BlockSpec((tm, LANES), lambda i: (i, 0))
        in_specs += [tab, tab, tab]
        args += list(rope)
    return pl.pallas_call(
        functools.partial(_nm_kernel, chunks=chunks, rope_cols=rope_cols),
        out_shape=jax.ShapeDtypeStruct((n, ncol), F32),
        grid=(n // tm,),
        in_specs=in_specs,
        out_specs=pl.BlockSpec((tm, ncol), lambda i: (i, 0)),
        scratch_shapes=[pltpu.VMEM((tm, d), BF16)],
        compiler_params=_params("arbitrary"),
        name="norm_mod_proj",
    )(*args)


def _moba_kernel(qt_ref, k_ref, vt_ref, o_ref, kb_scr, kmean_scr, sel_scr, *, nblk, n_sel, hb):
    i = pl.program_id(2)
    heads = range(hb)

    @pl.when(i == 0)
    def _():
        kmean_scr[...] = jnp.zeros_like(kmean_scr)
        for h in heads:
            for j in range(nblk):
                kj = k_ref[0, h, j]
                kmean_scr[h, j:j + 1, :] = jnp.mean(kj, axis=0, keepdims=True)
                kb_scr[h, j] = kj.astype(BF16)

    qs = []
    for h in heads:
        qt = qt_ref[0, h]
        gate = jnp.dot(kmean_scr[h], qt, precision=HIGHEST, preferred_element_type=F32)
        row = lax.broadcasted_iota(jnp.int32, gate.shape, 0)
        cnt = jnp.zeros(gate.shape, jnp.int32)
        for jp in range(nblk):
            gj = gate[jp:jp + 1, :]
            beats = (gj > gate) | ((gj == gate) & (jp < row))
            cnt = cnt + jnp.where(beats, 1, 0) * (jp < i).astype(jnp.int32)
        sel_scr[h] = jnp.where((row < i) & (cnt < n_sel), 1.0, 0.0)
        qs.append((qt * ATTN_SCALE).astype(BF16))

    s = [_dot(kb_scr[h, i], qs[h]) for h in heads]
    kidx = lax.broadcasted_iota(jnp.int32, s[0].shape, 0)
    qidx = lax.broadcasted_iota(jnp.int32, s[0].shape, 1)
    s = [jnp.where(kidx <= qidx, sh, -jnp.inf) for sh in s]
    m = [jnp.max(sh, axis=0, keepdims=True) for sh in s]
    p = [jnp.exp(s[h] - m[h]) for h in heads]
    l = [jnp.sum(ph, axis=0, keepdims=True) for ph in p]
    acc = [_dot(vt_ref[0, h, i], p[h].astype(BF16)) for h in heads]

    def body(j, carry):
        m, l, acc = carry
        s = [_dot(kb_scr[h, j], qs[h]) for h in heads]
        s = [jnp.where(sel_scr[h, pl.ds(j, 1), :] > 0.5, s[h], -jnp.inf) for h in heads]
        m_new = [jnp.maximum(m[h], jnp.max(s[h], axis=0, keepdims=True)) for h in heads]
        alpha = [jnp.exp(m[h] - m_new[h]) for h in heads]
        p = [jnp.exp(s[h] - m_new[h]) for h in heads]
        l = [l[h] * alpha[h] + jnp.sum(p[h], axis=0, keepdims=True) for h in heads]
        pv = [_dot(vt_ref[0, h, j], p[h].astype(BF16)) for h in heads]
        acc = [acc[h] * alpha[h] + pv[h] for h in heads]
        return tuple(m_new), tuple(l), tuple(acc)

    m, l, acc = lax.fori_loop(0, i, body, (tuple(m), tuple(l), tuple(acc)))
    for h in heads:
        o_ref[0, h] = acc[h] / l[h]


def _moba_attention(qt, kblk, vtblk):
    b, h, dh, t = qt.shape
    nblk, blk = kblk.shape[2], kblk.shape[3]
    hb = ATTN_HEADS_PER_STEP
    n_sel = min(MOBA_TOPK, max(nblk - 1, 1))
    gate_rows = -(-nblk // 8) * 8
    return pl.pallas_call(
        functools.partial(_moba_kernel, nblk=nblk, n_sel=n_sel, hb=hb),
        out_shape=jax.ShapeDtypeStruct((b, h, dh, t), F32),
        grid=(b, h // hb, nblk),
        in_specs=[
            pl.BlockSpec((1, hb, dh, blk), lambda bi, hi, i: (bi, hi, 0, i)),
            pl.BlockSpec((1, hb, nblk, blk, dh), lambda bi, hi, i: (bi, hi, 0, 0, 0)),
            pl.BlockSpec((1, hb, nblk, dh, blk), lambda bi, hi, i: (bi, hi, 0, 0, 0)),
        ],
        out_specs=pl.BlockSpec((1, hb, dh, blk), lambda bi, hi, i: (bi, hi, 0, i)),
        scratch_shapes=[
            pltpu.VMEM((hb, nblk, blk, dh), BF16),
            pltpu.VMEM((hb, gate_rows, dh), F32),
            pltpu.VMEM((hb, gate_rows, blk), F32),
        ],
        compiler_params=_params("arbitrary", "arbitrary", "arbitrary"),
        name="moba_attention",
    )(qt, kblk, vtblk)


def _sb_kernel(qt_ref, k_ref, vt_ref, o_ref, *, hb):
    i = pl.program_id(2)
    heads = range(hb)
    qs = [(qt_ref[0, h].astype(F32) * ATTN_SCALE).astype(BF16) for h in heads]
    blk, tq = k_ref.shape[3], qs[0].shape[1]
    kidx = lax.broadcasted_iota(jnp.int32, (blk, tq), 0)
    qidx = lax.broadcasted_iota(jnp.int32, (blk, tq), 1)
    causal = kidx < qidx
    r = lax.broadcasted_iota(jnp.int32, (blk, blk), 0)
    c = lax.broadcasted_iota(jnp.int32, (blk, blk), 1)
    later = jnp.where(c > r, 1.0, 0.0).astype(BF16)

    def block(j, tot, acc, diag):
        z = [_dot(k_ref[0, h, j], qs[h]) for h in heads]
        sp = [_softplus(zh) for zh in z]
        spm = [jnp.where(causal, sh, 0.0) for sh in sp] if diag else sp
        parts = [_split2(sh) for sh in spm]
        suf = [_dot(later, hi) + _dot(later, lo) for hi, lo in parts]
        w = [jnp.exp(z[h] - sp[h] - suf[h] - tot[h]) for h in heads]
        if diag:
            w = [jnp.where(causal, wh, 0.0) for wh in w]
        pv = [_dot(vt_ref[0, h, j], w[h].astype(BF16)) for h in heads]
        acc = tuple(acc[h] + pv[h] for h in heads)
        tot = tuple(tot[h] + suf[h][0:1, :] + spm[h][0:1, :] for h in heads)
        return tot, acc

    tot0 = tuple(jnp.zeros((1, tq), F32) for _ in heads)
    acc0 = tuple(jnp.zeros((qs[0].shape[0], tq), F32) for _ in heads)
    tot, acc = block(i, tot0, acc0, True)

    def body(step, ca):
        return block(i - 1 - step, ca[0], ca[1], False)

    tot, acc = lax.fori_loop(0, i, body, (tot, acc))
    for h in heads:
        o_ref[0, h] = acc[h]


def _sb_attention(qt, kblk, vtblk):
    b, h, dh, t = qt.shape
    nblk, blk = kblk.shape[2], kblk.shape[3]
    hb = ATTN_HEADS_PER_STEP
    return pl.pallas_call(
        functools.partial(_sb_kernel, hb=hb),
        out_shape=jax.ShapeDtypeStruct((b, h, dh, t), F32),
        grid=(b, h // hb, nblk),
        in_specs=[
            pl.BlockSpec((1, hb, dh, blk), lambda bi, hi, i: (bi, hi, 0, i)),
            pl.BlockSpec((1, hb, nblk, blk, dh), lambda bi, hi, i: (bi, hi, 0, 0, 0)),
            pl.BlockSpec((1, hb, nblk, dh, blk), lambda bi, hi, i: (bi, hi, 0, 0, 0)),
        ],
        out_specs=pl.BlockSpec((1, hb, dh, blk), lambda bi, hi, i: (bi, hi, 0, i)),
        compiler_params=_params("arbitrary", "arbitrary", "arbitrary"),
        name="sb_attention",
    )(qt, kblk, vtblk)


def _proj_res_kernel(o_ref, x_ref, mod_ref, w_ref, out_ref):
    y = _dot(o_ref[...], w_ref[...])
    out_ref[...] = x_ref[...] + mod_ref[0, GATE_M:GATE_M + 1, :] * y


def _proj_residual(o2, x2, mod, w, t):
    n, d = x2.shape
    kdim = o2.shape[1]
    tm = min(ROW_TILE, t)
    tpb = t // tm
    return pl.pallas_call(
        _proj_res_kernel,
        out_shape=jax.ShapeDtypeStruct((n, d), F32),
        grid=(n // tm,),
        in_specs=[
            pl.BlockSpec((tm, kdim), lambda i: (i, 0)),
            pl.BlockSpec((tm, d), lambda i: (i, 0)),
            pl.BlockSpec((1, 6, d), lambda i: (i // tpb, 0, 0)),
            pl.BlockSpec((kdim, d), lambda i: (0, 0), pipeline_mode=pl.Buffered(1)),
        ],
        out_specs=pl.BlockSpec((tm, d), lambda i: (i, 0)),
        compiler_params=_params("arbitrary"),
        name="proj_residual",
    )(o2, x2, mod, w)


def _ffn_kernel(x_ref, g_ref, mod_ref, wg_ref, wu_ref, wo_ref, out_ref, h_scr, a_scr, *, chunks):
    x = x_ref[...]
    h_scr[...] = _norm_mod(x, g_ref[...], mod_ref, SHIFT_F, SCALE_F).astype(BF16)
    for c0, cw in chunks:
        gate = _dot(h_scr[...], wg_ref[:, c0:c0 + cw])
        up = _dot(h_scr[...], wu_ref[:, c0:c0 + cw])
        a_scr[:, c0:c0 + cw] = (_silu(gate) * up).astype(BF16)
    y = _dot(a_scr[...], wo_ref[...])
    out_ref[...] = x + mod_ref[0, GATE_F:GATE_F + 1, :] * y


def _ffn(x2, g, mod, wg, wu, wo, t):
    n, d = x2.shape
    f = wg.shape[1]
    tm = min(ROW_TILE, t)
    tpb = t // tm
    chunks = _col_chunks(f, 256)
    const = lambda shape: pl.BlockSpec(shape, lambda i: (0, 0), pipeline_mode=pl.Buffered(1))
    return pl.pallas_call(
        functools.partial(_ffn_kernel, chunks=chunks),
        out_shape=jax.ShapeDtypeStruct((n, d), F32),
        grid=(n // tm,),
        in_specs=[
            pl.BlockSpec((tm, d), lambda i: (i, 0)),
            pl.BlockSpec((1, d), lambda i: (0, 0)),
            pl.BlockSpec((1, 6, d), lambda i: (i // tpb, 0, 0)),
            const((d, f)), const((d, f)), const((f, d)),
        ],
        out_specs=pl.BlockSpec((tm, d), lambda i: (i, 0)),
        scratch_shapes=[pltpu.VMEM((tm, d), BF16), pltpu.VMEM((tm, f), BF16)],
        compiler_params=_params("arbitrary"),
        name="swiglu_ffn",
    )(x2, g.reshape(1, d), mod, wg, wu, wo)


def _gdn_kernel(q_ref, k_ref, v_ref, z_ref, ab_ref, wq_ref, wk_ref, wv_ref, alog_ref, dtb_ref,
                ng_ref, o_ref, s_scr, halo_scr, buf_scr, *, tt):
    it = pl.program_id(1)
    ck, dk, nh = GDN_CHUNK, GDN_HEAD_DIM, GDN_HEADS
    nc = tt // ck
    chains = [(h, c) for c in range(nc) for h in range(nh)]

    @pl.when(it == 0)
    def _():
        s_scr[...] = jnp.zeros_like(s_scr)
        halo_scr[...] = jnp.zeros_like(halo_scr)

    def conv_silu(x_ref, w_ref, idx):
        x = x_ref[0]
        buf_scr[idx, 0:HALO_ROWS, :] = halo_scr[idx]
        buf_scr[idx, HALO_ROWS:HALO_ROWS + tt, :] = x
        halo_scr[idx] = x[tt - HALO_ROWS:tt, :]
        y = jnp.zeros_like(x)
        for j in range(GDN_CONV):
            y = y + w_ref[j:j + 1, :] * buf_scr[idx, pl.ds(HALO_ROWS - (GDN_CONV - 1) + j, tt), :]
        return _silu(y)

    qa, ka, va = conv_silu(q_ref, wq_ref, 0), conv_silu(k_ref, wk_ref, 1), conv_silu(v_ref, wv_ref, 2)

    def head(x, h):
        return x[:, h * dk:(h + 1) * dk]

    def l2norm(x):
        return x * lax.rsqrt(jnp.sum(x * x, axis=-1, keepdims=True) + EPS)

    qn = [l2norm(head(qa, h)) * (dk ** -0.5) for h in range(nh)]
    kn = [l2norm(head(ka, h)) for h in range(nh)]

    ab = ab_ref[0]
    g_all = -jnp.exp(alog_ref[...]) * _softplus(ab + dtb_ref[...])
    beta_all = jax.nn.sigmoid(ab)
    ri = lax.broadcasted_iota(jnp.int32, (tt, tt), 0)
    ci = lax.broadcasted_iota(jnp.int32, (tt, tt), 1)
    same_chunk_lower = jnp.where((ri >= ci) & (ri // ck == ci // ck), 1.0, 0.0).astype(BF16)
    g1 = g_all.astype(BF16)
    r1 = g_all - g1.astype(F32)
    g2 = r1.astype(BF16)
    g3 = (r1 - g2.astype(F32)).astype(BF16)
    gc_all = _dot(same_chunk_lower, g1) + _dot(same_chunk_lower, g2) + _dot(same_chunk_lower, g3)
    gc_t = gc_all.T

    r64 = lax.broadcasted_iota(jnp.int32, (ck, ck), 0)
    c64 = lax.broadcasted_iota(jnp.int32, (ck, ck), 1)
    nt = (((1,), (1,)), ((), ()))
    tn = (((0,), (0,)), ((), ()))

    gcb, egc, kbeta, a_mat, attn, rhs = {}, {}, {}, {}, {}, {}
    raw = {}
    for h, c in chains:
        rows = slice(c * ck, (c + 1) * ck)
        gcb[h, c] = jnp.broadcast_to(gc_all[rows, h:h + 1], (ck, dk))
        betab = jnp.broadcast_to(beta_all[rows, nh + h:nh + h + 1], (ck, dk))
        egc[h, c] = jnp.exp(gcb[h, c])
        kc = kn[h][rows]
        kbeta[h, c] = kc * betab
        rhs[h, c] = jnp.concatenate([head(va, h)[rows] * betab, kbeta[h, c] * egc[h, c]], axis=1)
        lhs = jnp.concatenate([kbeta[h, c], qn[h][rows]], axis=0).astype(BF16)
        raw[h, c] = lax.dot_general(lhs, kc.astype(BF16), nt, preferred_element_type=F32)
    for h, c in chains:
        grow = gc_t[h:h + 1, c * ck:(c + 1) * ck]
        decay = jnp.where(r64 >= c64, jnp.exp(gcb[h, c][:, :ck] - grow), 0.0)
        a_mat[h, c] = jnp.where(r64 > c64, raw[h, c][:ck] * decay, 0.0)
        attn[h, c] = (raw[h, c][ck:] * decay).astype(BF16)

    def mm3(ah, al, bh, bl):
        return _dot(ah, bh) + _dot(ah, bl) + _dot(al, bh)

    ps = {k: _split2(a_mat[k]) for k in chains}
    ys = {k: _split2(rhs[k]) for k in chains}
    y = {k: rhs[k] - mm3(*ps[k], *ys[k]) for k in chains}
    for _ in range((ck - 1).bit_length() - 1):
        pw = {k: mm3(*ps[k], *ps[k]) for k in chains}
        ps = {k: _split2(pw[k]) for k in chains}
        ys = {k: _split2(y[k]) for k in chains}
        y = {k: y[k] + mm3(*ps[k], *ys[k]) for k in chains}

    state = [s_scr[h] for h in range(nh)]
    for c in range(nc):
        rows = slice(c * ck, (c + 1) * ck)
        s16 = [state[h].astype(BF16) for h in range(nh)]
        v_new = [y[h, c][:, :dk] - _dot(y[h, c][:, dk:].astype(BF16), s16[h]) for h in range(nh)]
        vn16 = [v.astype(BF16) for v in v_new]
        o_c = [_dot((qn[h][rows] * egc[h, c]).astype(BF16), s16[h]) + _dot(attn[h, c], vn16[h])
               for h in range(nh)]
        for h in range(nh):
            g_last = gcb[h, c][ck - 1:ck, :]
            k_dec = kn[h][rows] * jnp.exp(g_last - gcb[h, c])
            state[h] = state[h] * jnp.exp(g_last) + lax.dot_general(
                k_dec.astype(BF16), vn16[h], tn, preferred_element_type=F32)
        for h in range(nh):
            o = o_c[h]
            on = o * lax.rsqrt(jnp.mean(o * o, axis=-1, keepdims=True) + EPS) * ng_ref[...]
            cols = slice(h * dk, (h + 1) * dk)
            o_ref[0, rows, cols] = (on * _silu(z_ref[0, rows, cols])).astype(o_ref.dtype)
    for h in range(nh):
        s_scr[h] = state[h]


def _gated_deltanet(proj3, conv_w, a_log, dt_bias, norm_g):
    b, t, _ = proj3.shape
    nh, dk = GDN_HEADS, GDN_HEAD_DIM
    gw = nh * dk
    tt = min(GDN_ROW_TILE, t)
    col = lambda j: pl.BlockSpec((1, tt, gw), lambda bi, i: (bi, i, j))
    wcol = lambda j: pl.BlockSpec((GDN_CONV, gw), lambda bi, i: (0, j))
    row = pl.BlockSpec((1, LANES), lambda bi, i: (0, 0))
    pad = lambda v: jnp.pad(v.astype(F32), (0, LANES - v.shape[0])).reshape(1, LANES)
    return pl.pallas_call(
        functools.partial(_gdn_kernel, tt=tt),
        out_shape=jax.ShapeDtypeStruct((b, t, gw), BF16),
        grid=(b, t // tt),
        in_specs=[
            col(0), col(1), col(2), col(3),
            pl.BlockSpec((1, tt, LANES), lambda bi, i: (bi, i, 4 * nh)),
            wcol(0), wcol(1), wcol(2),
            row, row, pl.BlockSpec((1, dk), lambda bi, i: (0, 0)),
        ],
        out_specs=pl.BlockSpec((1, tt, gw), lambda bi, i: (bi, i, 0)),
        scratch_shapes=[
            pltpu.VMEM((nh, dk, dk), F32),
            pltpu.VMEM((3, HALO_ROWS, gw), F32),
            pltpu.VMEM((3, HALO_ROWS + tt, gw), F32),
        ],
        compiler_params=_params("arbitrary", "arbitrary"),
        name="gated_deltanet",
    )(proj3, proj3, proj3, proj3, proj3, conv_w, conv_w, conv_w,
      pad(a_log), pad(dt_bias), norm_g.astype(F32).reshape(1, dk))


def _final_norm_kernel(x_ref, g_ref, o_ref):
    x = x_ref[...]
    o_ref[...] = x * lax.rsqrt(jnp.mean(x * x, axis=-1, keepdims=True) + EPS) * g_ref[...]


def _final_norm(x2, g):
    n, d = x2.shape
    tm = min(ROW_TILE, n)
    return pl.pallas_call(
        _final_norm_kernel,
        out_shape=jax.ShapeDtypeStruct((n, d), F32),
        grid=(n // tm,),
        in_specs=[pl.BlockSpec((tm, d), lambda i: (i, 0)), pl.BlockSpec((1, d), lambda i: (0, 0))],
        out_specs=pl.BlockSpec((tm, d), lambda i: (i, 0)),
        compiler_params=_params("arbitrary"),
        name="final_norm",
    )(x2, g.reshape(1, d))


def _attention_layer(x2, g, mod, w_in, w_out, rope, b, t):
    mw = MOBA_HEADS * HEAD_DIM
    sw = SB_HEADS * HEAD_DIM
    nblk = t // MOBA_BLOCK
    proj = _norm_mod_proj(x2, g, mod, w_in.astype(BF16), t, rope=rope, rope_cols=2 * mw)

    def heads(c0, nh):
        return proj[:, c0:c0 + nh * HEAD_DIM].reshape(b, nblk, MOBA_BLOCK, nh, HEAD_DIM)

    def feature_major(c0, nh, dtype):
        return heads(c0, nh).transpose(0, 3, 4, 1, 2).reshape(b, nh, HEAD_DIM, t).astype(dtype)

    def key_blocks(c0, nh, dtype):
        return heads(c0, nh).transpose(0, 3, 1, 2, 4).astype(dtype)

    def value_blocks(c0, nh):
        return heads(c0, nh).transpose(0, 3, 1, 4, 2).astype(BF16)

    oa = _moba_attention(feature_major(0, MOBA_HEADS, F32), key_blocks(mw, MOBA_HEADS, F32),
                         value_blocks(2 * mw, MOBA_HEADS))
    s0 = 3 * mw
    ob = _sb_attention(feature_major(s0, SB_HEADS, BF16), key_blocks(s0 + sw, SB_HEADS, BF16),
                       value_blocks(s0 + 2 * sw, SB_HEADS))
    o = jnp.concatenate([oa, ob], axis=1)
    o2 = o.transpose(0, 3, 1, 2).reshape(b * t, mw + sw).astype(BF16)
    return _proj_residual(o2, x2, mod, w_out.astype(BF16), t)


def _gdn_layer(x2, g, mod, w_in, conv_w, a_log, dt_bias, norm_g, w_out, b, t):
    gw = GDN_HEADS * GDN_HEAD_DIM
    w_main = w_in[:, :4 * gw]
    w_ab = jnp.pad(w_in[:, 4 * gw:], ((0, 0), (0, LANES - 2 * GDN_HEADS)))
    w_cat = jnp.concatenate([w_main, w_ab], axis=1).astype(BF16)
    proj = _norm_mod_proj(x2, g, mod, w_cat, t, chunk=384)
    o = _gated_deltanet(proj.reshape(b, t, -1), conv_w, a_log, dt_bias, norm_g)
    return _proj_residual(o.reshape(b * t, gw), x2, mod, w_out.astype(BF16), t)


def kernel(x, c, positions, ada_w, ada_b, norm_mix_g, norm_ffn_g, attn_w_in, attn_w_out, gdn_w_in,
           gdn_conv_w, gdn_a_log, gdn_dt_bias, gdn_norm_g, gdn_w_out, ffn_w_in, ffn_w_out,
           final_norm_g):
    b, t, d = x.shape
    depth = ada_w.shape[0]
    assert t % MOBA_BLOCK == 0 and t % GDN_CHUNK == 0
    mods = _modulation(c, ada_w, ada_b)
    rope = _rope_tables(positions)
    x2 = x.reshape(b * t, d)
    f = ffn_w_out.shape[1]
    for layer in range(depth):
        mod = mods[layer].reshape(b, 6, d)
        i = layer // 2
        if layer % 2 == 0:
            x2 = _attention_layer(x2, norm_mix_g[layer], mod, attn_w_in[i], attn_w_out[i], rope, b, t)
        else:
            x2 = _gdn_layer(x2, norm_mix_g[layer], mod, gdn_w_in[i], gdn_conv_w[i], gdn_a_log[i],
                            gdn_dt_bias[i], gdn_norm_g[i], gdn_w_out[i], b, t)
        w_in = ffn_w_in[layer].astype(BF16)
        x2 = _ffn(x2, norm_ffn_g[layer], mod, w_in[:, :f], w_in[:, f:], ffn_w_out[layer].astype(BF16), t)
    return _final_norm(x2, final_norm_g).reshape(b, t, d)
```

```python
import functools

import jax
import jax.numpy as jnp
from jax import lax
from jax.experimental import pallas as pl
from jax.experimental.pallas import tpu as pltpu

F32 = jnp.float32
BF16 = jnp.bfloat16
HIGHEST = lax.Precision.HIGHEST

EPS = 1e-6
HEAD_DIM = 64
MOBA_HEADS = 8
SB_HEADS = 8
MOBA_BLOCK = 256
MOBA_TOPK = 3
ROPE_THETA = 500000.0
ROPE_DIMS = HEAD_DIM // 4
GDN_HEADS = 8
GDN_HEAD_DIM = 128
GDN_CONV = 4
GDN_CHUNK = 64
ATTN_SCALE = HEAD_DIM ** -0.5

LANES = 128
HALO_ROWS = 8
V7X_VMEM_LIMIT = 56 * 1024 * 1024
ROW_TILE = 512
MOD_COL_TILE = 1536
GDN_ROW_TILE = 128
ATTN_HEADS_PER_STEP = 4
HEADS_PER_LANE_GROUP = LANES // HEAD_DIM
SB_UNDERFLOW_LOG = 110.0

SHIFT_M, SCALE_M, GATE_M, SHIFT_F, SCALE_F, GATE_F = range(6)


def _params(*sem):
    return pltpu.CompilerParams(dimension_semantics=sem, vmem_limit_bytes=V7X_VMEM_LIMIT)


def _silu(x):
    return x * jax.nn.sigmoid(x)


def _softplus(x):
    return jnp.maximum(x, 0.0) + jnp.log(1.0 + jnp.exp(-jnp.abs(x)))


def _dot(a, b):
    return jnp.dot(a, b, preferred_element_type=F32)


def _split2(x):
    hi = x.astype(BF16)
    return hi, (x - hi.astype(F32)).astype(BF16)


def _col_chunks(total, width):
    chunks, c0 = [], 0
    while c0 < total:
        cw = min(width, total - c0)
        chunks.append((c0, cw))
        c0 += cw
    return tuple(chunks)


def _mod_kernel(c_ref, w_ref, b_ref, o_ref):
    cond = _silu(c_ref[...])
    o_ref[0] = jnp.dot(cond, w_ref[0], precision=HIGHEST, preferred_element_type=F32) + b_ref[0]


def _modulation(c, ada_w, ada_b):
    depth, d, six_d = ada_w.shape
    b = c.shape[0]
    tn = MOD_COL_TILE
    return pl.pallas_call(
        _mod_kernel,
        out_shape=jax.ShapeDtypeStruct((depth, b, six_d), F32),
        grid=(depth, six_d // tn),
        in_specs=[
            pl.BlockSpec((b, d), lambda l, j: (0, 0)),
            pl.BlockSpec((1, d, tn), lambda l, j: (l, 0, j)),
            pl.BlockSpec((1, 1, tn), lambda l, j: (l, 0, j)),
        ],
        out_specs=pl.BlockSpec((1, b, tn), lambda l, j: (l, 0, j)),
        compiler_params=_params("arbitrary", "arbitrary"),
        name="adaln_mod",
    )(c, ada_w, ada_b.reshape(depth, 1, six_d))


def _rope_kernel(pos_ref, freq_ref, ma_ref, mb_ref, c_ref, sa_ref, sb_ref):
    ang = pos_ref[...] * freq_ref[...]
    s = jnp.sin(ang)
    c_ref[...] = jnp.cos(ang)
    sa_ref[...] = s * ma_ref[...]
    sb_ref[...] = s * mb_ref[...]


def _rope_tables(positions):
    n = positions.size
    half = ROPE_DIMS // 2
    inv_freq = ROPE_THETA ** (-jnp.arange(half, dtype=F32) * 2.0 / ROPE_DIMS)
    lane = jnp.arange(LANES) % HEAD_DIM
    freq = jnp.where(lane < ROPE_DIMS, inv_freq[lane % half], 0.0).astype(F32)[None, :]
    ma = jnp.where(lane < half, -1.0, 0.0).astype(F32)[None, :]
    mb = jnp.where((lane >= half) & (lane < ROPE_DIMS), 1.0, 0.0).astype(F32)[None, :]
    pos = positions.astype(F32).reshape(n, 1)
    tm = min(ROW_TILE, n)
    row = pl.BlockSpec((1, LANES), lambda i: (0, 0))
    tab = pl.BlockSpec((tm, LANES), lambda i: (i, 0))
    return pl.pallas_call(
        _rope_kernel,
        out_shape=[jax.ShapeDtypeStruct((n, LANES), F32)] * 3,
        grid=(n // tm,),
        in_specs=[pl.BlockSpec((tm, 1), lambda i: (i, 0)), row, row, row],
        out_specs=[tab, tab, tab],
        compiler_params=_params("arbitrary"),
        name="rope_tables",
    )(pos, freq, ma, mb)


def _norm_mod(x, g_row, mod_ref, shift_idx, scale_idx):
    y = x * lax.rsqrt(jnp.mean(x * x, axis=-1, keepdims=True) + EPS) * g_row
    return y * (1.0 + mod_ref[0, scale_idx:scale_idx + 1, :]) + mod_ref[0, shift_idx:shift_idx + 1, :]


def _nm_kernel(*refs, chunks, rope_cols, out_cols):
    n_out = len(out_cols)
    x_ref, g_ref, mod_ref, w_ref = refs[:4]
    if rope_cols:
        c_ref, sa_ref, sb_ref = refs[4:7]
    out_refs, h_scr = refs[-1 - n_out:-1], refs[-1]
    h = _norm_mod(x_ref[...], g_ref[...], mod_ref, SHIFT_M, SCALE_M)
    h_scr[...] = h.astype(BF16)
    for c0, cw in chunks:
        o_idx = max(k for k, start in enumerate(out_cols) if start <= c0)
        o_ref, oc0 = out_refs[o_idx], c0 - out_cols[o_idx]
        acc = _dot(h_scr[...], w_ref[:, c0:c0 + cw])
        if c0 < rope_cols:
            reps = cw // LANES
            cos = jnp.tile(c_ref[...], (1, reps))
            sa = jnp.tile(sa_ref[...], (1, reps))
            sb = jnp.tile(sb_ref[...], (1, reps))
            half = ROPE_DIMS // 2
            acc = (acc * cos + pltpu.roll(acc, cw - half, axis=1) * sa
                   + pltpu.roll(acc, half, axis=1) * sb)
        o_ref[:, oc0:oc0 + cw] = acc.astype(o_ref.dtype)


def _norm_mod_proj(x2, g, mod, w, t, *, outs, rope=None, rope_cols=0, chunk=512):
    n, d = x2.shape
    ncol = w.shape[1]
    tm = min(ROW_TILE, t)
    tpb = t // tm
    out_cols = tuple(c0 for c0, _ in outs)
    widths = [end - c0 for c0, end in zip(out_cols, out_cols[1:] + (ncol,))]
    chunks = tuple((out_cols[k] + c0, cw) for k, wd in enumerate(widths) for c0, cw in _col_chunks(wd, chunk))
    assert all(cw % LANES == 0 for _, cw in chunks)
    assert all(c0 + cw <= rope_cols or c0 >= rope_cols for c0, cw in chunks)
    in_specs = [
        pl.BlockSpec((tm, d), lambda i: (i, 0)),
        pl.BlockSpec((1, d), lambda i: (0, 0)),
        pl.BlockSpec((1, 6, d), lambda i: (i // tpb, 0, 0)),
        pl.BlockSpec((d, ncol), lambda i: (0, 0), pipeline_mode=pl.Buffered(1)),
    ]
    args = [x2, g.reshape(1, d), mod, w]
    if rope_cols:
        tab = pl.BlockSpec((tm, LANES), lambda i: (i, 0))
        in_specs += [tab, tab, tab]
        args += list(rope)
    return pl.pallas_call(
        functools.partial(_nm_kernel, chunks=chunks, rope_cols=rope_cols, out_cols=out_cols),
        out_shape=[jax.ShapeDtypeStruct((n, wd), dt) for wd, (_, dt) in zip(widths, outs)],
        grid=(n // tm,),
        in_specs=in_specs,
        out_specs=[pl.BlockSpec((tm, wd), lambda i: (i, 0)) for wd in widths],
        scratch_shapes=[pltpu.VMEM((tm, d), BF16)],
        compiler_params=_params("arbitrary"),
        name="norm_mod_proj",
    )(*args)


def _pair_heads(npair):
    return [(p, a) for p in range(npair) for a in range(HEADS_PER_LANE_GROUP)]


def _pair_cols(p):
    return slice(p * LANES, (p + 1) * LANES)


def _head_rows(a):
    return slice(a * HEAD_DIM, (a + 1) * HEAD_DIM)


def _split_pair_queries(qt2):
    owner = lax.broadcasted_iota(jnp.int32, qt2.shape, 0) // HEAD_DIM
    return [jnp.where(owner == a, qt2, 0.0) for a in range(HEADS_PER_LANE_GROUP)]


def _moba_kernel(q_ref, k_ref, v_ref, o_ref, kb_scr, vt_scr, kmean_scr, sel_scr, *,
                 nblk, blk, n_sel, npair):
    i = pl.program_id(2)
    heads = _pair_heads(npair)

    @pl.when(i == 0)
    def _():
        kmean_scr[...] = jnp.zeros_like(kmean_scr)
        for p in range(npair):
            for j in range(nblk):
                rows = slice(j * blk, (j + 1) * blk)
                kj = k_ref[0, rows, _pair_cols(p)]
                kmean_scr[p, j:j + 1, :] = jnp.mean(kj, axis=0, keepdims=True)
                kb_scr[p, j] = kj.astype(BF16)
                vt_scr[p, j] = v_ref[0, rows, _pair_cols(p)].astype(F32).T.astype(BF16)

    qs = {}
    for p in range(npair):
        for a, qa in enumerate(_split_pair_queries(q_ref[0, :, _pair_cols(p)].T)):
            gate = jnp.dot(kmean_scr[p], qa, precision=HIGHEST, preferred_element_type=F32)
            row = lax.broadcasted_iota(jnp.int32, gate.shape, 0)
            cnt = jnp.zeros(gate.shape, jnp.int32)
            for jp in range(nblk):
                gj = gate[jp:jp + 1, :]
                beats = (gj > gate) | ((gj == gate) & (jp < row))
                cnt = cnt + jnp.where(beats, 1, 0) * (jp < i).astype(jnp.int32)
            sel_scr[p * HEADS_PER_LANE_GROUP + a] = jnp.where((row < i) & (cnt < n_sel), 1.0, 0.0)
            qs[p, a] = (qa * ATTN_SCALE).astype(BF16)

    def values_t(p, a, j):
        return vt_scr[p, j, _head_rows(a), :]

    s = {h: _dot(kb_scr[h[0], i], qs[h]) for h in heads}
    kidx = lax.broadcasted_iota(jnp.int32, (blk, blk), 0)
    qidx = lax.broadcasted_iota(jnp.int32, (blk, blk), 1)
    s = {h: jnp.where(kidx <= qidx, s[h], -jnp.inf) for h in heads}
    m = {h: jnp.max(s[h], axis=0, keepdims=True) for h in heads}
    p_ = {h: jnp.exp(s[h] - m[h]) for h in heads}
    l = {h: jnp.sum(p_[h], axis=0, keepdims=True) for h in heads}
    acc = {h: _dot(values_t(*h, i), p_[h].astype(BF16)) for h in heads}

    def body(j, carry):
        m, l, acc = (dict(zip(heads, c)) for c in carry)
        s = {h: _dot(kb_scr[h[0], j], qs[h]) for h in heads}
        s = {h: jnp.where(sel_scr[h[0] * HEADS_PER_LANE_GROUP + h[1], pl.ds(j, 1), :] > 0.5,
                          s[h], -jnp.inf) for h in heads}
        m_new = {h: jnp.maximum(m[h], jnp.max(s[h], axis=0, keepdims=True)) for h in heads}
        alpha = {h: jnp.exp(m[h] - m_new[h]) for h in heads}
        p_ = {h: jnp.exp(s[h] - m_new[h]) for h in heads}
        l = {h: l[h] * alpha[h] + jnp.sum(p_[h], axis=0, keepdims=True) for h in heads}
        pv = {h: _dot(values_t(*h, j), p_[h].astype(BF16)) for h in heads}
        acc = {h: acc[h] * alpha[h] + pv[h] for h in heads}
        return tuple(tuple(d[h] for h in heads) for d in (m_new, l, acc))

    carry = tuple(tuple(d[h] for h in heads) for d in (m, l, acc))
    m, l, acc = (dict(zip(heads, c)) for c in lax.fori_loop(0, i, body, carry))
    for p in range(npair):
        o_t = jnp.concatenate([acc[p, a] / l[p, a] for a in range(HEADS_PER_LANE_GROUP)], axis=0)
        o_ref[0, :, _pair_cols(p)] = o_t.T.astype(o_ref.dtype)


def _moba_attention(qk3, v3, v_col0):
    b, t, w2 = qk3.shape
    w = w2 // 2
    blk = MOBA_BLOCK
    nblk = t // blk
    hb = ATTN_HEADS_PER_STEP
    npair = hb // HEADS_PER_LANE_GROUP
    gw = hb * HEAD_DIM
    n_sel = min(MOBA_TOPK, max(nblk - 1, 1))
    gate_rows = -(-nblk // 8) * 8
    return pl.pallas_call(
        functools.partial(_moba_kernel, nblk=nblk, blk=blk, n_sel=n_sel, npair=npair),
        out_shape=jax.ShapeDtypeStruct((b, t, w), BF16),
        grid=(b, w // gw, nblk),
        in_specs=[
            pl.BlockSpec((1, blk, gw), lambda bi, hi, i: (bi, i, hi)),
            pl.BlockSpec((1, t, gw), lambda bi, hi, i: (bi, 0, w // gw + hi)),
            pl.BlockSpec((1, t, gw), lambda bi, hi, i: (bi, 0, v_col0 // gw + hi)),
        ],
        out_specs=pl.BlockSpec((1, blk, gw), lambda bi, hi, i: (bi, i, hi)),
        scratch_shapes=[
            pltpu.VMEM((npair, nblk, blk, LANES), BF16),
            pltpu.VMEM((npair, nblk, LANES, blk), BF16),
            pltpu.VMEM((npair, gate_rows, LANES), F32),
            pltpu.VMEM((hb, gate_rows, blk), F32),
        ],
        compiler_params=_params("arbitrary", "arbitrary", "arbitrary"),
        name="moba_attention",
    )(qk3, qk3, v3)


def _sb_kernel(q_ref, k_ref, v_ref, o_ref, vt_scr, *, nblk, blk, npair):
    i = pl.program_id(2)
    heads = _pair_heads(npair)

    @pl.when(i == 0)
    def _():
        for p in range(npair):
            for j in range(nblk):
                rows = slice(j * blk, (j + 1) * blk)
                vt_scr[p, j] = v_ref[0, rows, _pair_cols(p)].astype(F32).T.astype(BF16)

    qs = {}
    for p in range(npair):
        for a, qa in enumerate(_split_pair_queries(q_ref[0, :, _pair_cols(p)].astype(F32).T)):
            qs[p, a] = (qa * ATTN_SCALE).astype(BF16)
    kidx = lax.broadcasted_iota(jnp.int32, (blk, blk), 0)
    qidx = lax.broadcasted_iota(jnp.int32, (blk, blk), 1)
    causal = kidx < qidx
    later = jnp.where(qidx > kidx, 1.0, 0.0).astype(BF16)

    def block(j, tot, acc, diag):
        rows = pl.ds(pl.multiple_of(j * blk, blk), blk)
        z = {h: _dot(k_ref[0, rows, _pair_cols(h[0])], qs[h]) for h in heads}
        sp = {h: _softplus(z[h]) for h in heads}
        spm = {h: jnp.where(causal, sp[h], 0.0) for h in heads} if diag else sp
        parts = {h: _split2(spm[h]) for h in heads}
        suf = {h: _dot(later, parts[h][0]) + _dot(later, parts[h][1]) for h in heads}
        w = {h: jnp.exp(z[h] - sp[h] - suf[h] - tot[h]) for h in heads}
        if diag:
            w = {h: jnp.where(causal, w[h], 0.0) for h in heads}
        pv = {h: _dot(vt_scr[h[0], j, _head_rows(h[1]), :], w[h].astype(BF16)) for h in heads}
        acc = {h: acc[h] + pv[h] for h in heads}
        tot = {h: tot[h] + suf[h][0:1, :] + spm[h][0:1, :] for h in heads}
        return tot, acc

    tot = {h: jnp.zeros((1, blk), F32) for h in heads}
    acc = {h: jnp.zeros((HEAD_DIM, blk), F32) for h in heads}
    tot, acc = block(i, tot, acc, True)

    def pack(tot, acc):
        return tuple(tot[h] for h in heads), tuple(acc[h] for h in heads)

    def alive(carry):
        step, tot, _ = carry
        least = functools.reduce(jnp.minimum, tot)
        return (step < i) & (jnp.min(least) < SB_UNDERFLOW_LOG)

    def body(carry):
        step, tot, acc = carry
        tot, acc = block(i - 1 - step, dict(zip(heads, tot)), dict(zip(heads, acc)), False)
        return (step + 1, *pack(tot, acc))

    _, _, acc = lax.while_loop(alive, body, (jnp.int32(0), *pack(tot, acc)))
    acc = dict(zip(heads, acc))
    for p in range(npair):
        o_t = jnp.concatenate([acc[p, a] for a in range(HEADS_PER_LANE_GROUP)], axis=0)
        o_ref[0, :, _pair_cols(p)] = o_t.T.astype(o_ref.dtype)


def _sb_attention(qkv3, col0, w):
    b, t, _ = qkv3.shape
    blk = MOBA_BLOCK
    nblk = t // blk
    hb = ATTN_HEADS_PER_STEP
    npair = hb // HEADS_PER_LANE_GROUP
    gw = hb * HEAD_DIM
    first = lambda k: (col0 + k * w) // gw
    return pl.pallas_call(
        functools.partial(_sb_kernel, nblk=nblk, blk=blk, npair=npair),
        out_shape=jax.ShapeDtypeStruct((b, t, w), BF16),
        grid=(b, w // gw, nblk),
        in_specs=[
            pl.BlockSpec((1, blk, gw), lambda bi, hi, i: (bi, i, first(0) + hi)),
            pl.BlockSpec((1, t, gw), lambda bi, hi, i: (bi, 0, first(1) + hi)),
            pl.BlockSpec((1, t, gw), lambda bi, hi, i: (bi, 0, first(2) + hi)),
        ],
        out_specs=pl.BlockSpec((1, blk, gw), lambda bi, hi, i: (bi, i, hi)),
        scratch_shapes=[pltpu.VMEM((npair, nblk, LANES, blk), BF16)],
        compiler_params=_params("arbitrary", "arbitrary", "arbitrary"),
        name="sb_attention",
    )(qkv3, qkv3, qkv3)


def _proj_res_kernel(*refs, splits):
    o_refs = refs[:len(splits)]
    x_ref, mod_ref, w_ref, out_ref = refs[len(splits):]
    y = None
    for o_ref, (r0, r1) in zip(o_refs, splits):
        part = _dot(o_ref[...], w_ref[r0:r1, :])
        y = part if y is None else y + part
    out_ref[...] = x_ref[...] + mod_ref[0, GATE_M:GATE_M + 1, :] * y


def _proj_residual(o_parts, x2, mod, w, t):
    n, d = x2.shape
    tm = min(ROW_TILE, t)
    tpb = t // tm
    splits, r0 = [], 0
    for o in o_parts:
        splits.append((r0, r0 + o.shape[1]))
        r0 += o.shape[1]
    assert r0 == w.shape[0]
    return pl.pallas_call(
        functools.partial(_proj_res_kernel, splits=tuple(splits)),
        out_shape=jax.ShapeDtypeStruct((n, d), F32),
        grid=(n // tm,),
        in_specs=[pl.BlockSpec((tm, o.shape[1]), lambda i: (i, 0)) for o in o_parts] + [
            pl.BlockSpec((tm, d), lambda i: (i, 0)),
            pl.BlockSpec((1, 6, d), lambda i: (i // tpb, 0, 0)),
            pl.BlockSpec((w.shape[0], d), lambda i: (0, 0), pipeline_mode=pl.Buffered(1)),
        ],
        out_specs=pl.BlockSpec((tm, d), lambda i: (i, 0)),
        compiler_params=_params("arbitrary"),
        name="proj_residual",
    )(*o_parts, x2, mod, w)


def _ffn_kernel(x_ref, g_ref, mod_ref, wg_ref, wu_ref, wo_ref, out_ref, h_scr, a_scr, *, chunks):
    x = x_ref[...]
    h_scr[...] = _norm_mod(x, g_ref[...], mod_ref, SHIFT_F, SCALE_F).astype(BF16)
    for c0, cw in chunks:
        gate = _dot(h_scr[...], wg_ref[:, c0:c0 + cw])
        up = _dot(h_scr[...], wu_ref[:, c0:c0 + cw])
        a_scr[:, c0:c0 + cw] = (_silu(gate) * up).astype(BF16)
    y = _dot(a_scr[...], wo_ref[...])
    out_ref[...] = x + mod_ref[0, GATE_F:GATE_F + 1, :] * y


def _ffn(x2, g, mod, wg, wu, wo, t):
    n, d = x2.shape
    f = wg.shape[1]
    tm = min(ROW_TILE, t)
    tpb = t // tm
    chunks = _col_chunks(f, 256)
    const = lambda shape: pl.BlockSpec(shape, lambda i: (0, 0), pipeline_mode=pl.Buffered(1))
    return pl.pallas_call(
        functools.partial(_ffn_kernel, chunks=chunks),
        out_shape=jax.ShapeDtypeStruct((n, d), F32),
        grid=(n // tm,),
        in_specs=[
            pl.BlockSpec((tm, d), lambda i: (i, 0)),
            pl.BlockSpec((1, d), lambda i: (0, 0)),
            pl.BlockSpec((1, 6, d), lambda i: (i // tpb, 0, 0)),
            const((d, f)), const((d, f)), const((f, d)),
        ],
        out_specs=pl.BlockSpec((tm, d), lambda i: (i, 0)),
        scratch_shapes=[pltpu.VMEM((tm, d), BF16), pltpu.VMEM((tm, f), BF16)],
        compiler_params=_params("arbitrary"),
        name="swiglu_ffn",
    )(x2, g.reshape(1, d), mod, wg, wu, wo)


def _gdn_kernel(q_ref, k_ref, v_ref, z_ref, ab_ref, wq_ref, wk_ref, wv_ref, alog_ref, dtb_ref,
                ng_ref, o_ref, s_scr, halo_scr, *, tt):
    it = pl.program_id(1)
    ck, dk, nh = GDN_CHUNK, GDN_HEAD_DIM, GDN_HEADS
    nc = tt // ck
    chains = [(h, c) for c in range(nc) for h in range(nh)]

    @pl.when(it == 0)
    def _():
        s_scr[...] = jnp.zeros_like(s_scr)
        halo_scr[...] = jnp.zeros_like(halo_scr)

    def conv_silu(x_ref, w_ref, idx):
        x = x_ref[0]
        prev = halo_scr[idx]
        halo_scr[idx] = x[tt - HALO_ROWS:tt, :]
        row = lax.broadcasted_iota(jnp.int32, prev.shape, 0)
        y = w_ref[GDN_CONV - 1:GDN_CONV, :] * x
        for k in range(1, GDN_CONV):
            rolled = pltpu.roll(x, k, axis=0)
            first = jnp.where(row < k, pltpu.roll(prev, k, axis=0), rolled[:HALO_ROWS])
            shifted = jnp.concatenate([first, rolled[HALO_ROWS:]], axis=0)
            y = y + w_ref[GDN_CONV - 1 - k:GDN_CONV - k, :] * shifted
        return _silu(y)

    qa, ka, va = conv_silu(q_ref, wq_ref, 0), conv_silu(k_ref, wk_ref, 1), conv_silu(v_ref, wv_ref, 2)

    def head(x, h):
        return x[:, h * dk:(h + 1) * dk]

    def l2norm(x):
        return x * lax.rsqrt(jnp.sum(x * x, axis=-1, keepdims=True) + EPS)

    qn = [l2norm(head(qa, h)) * (dk ** -0.5) for h in range(nh)]
    kn = [l2norm(head(ka, h)) for h in range(nh)]

    ab = ab_ref[0]
    g_all = -jnp.exp(alog_ref[...]) * _softplus(ab + dtb_ref[...])
    beta_all = jax.nn.sigmoid(ab)
    ri = lax.broadcasted_iota(jnp.int32, (tt, tt), 0)
    ci = lax.broadcasted_iota(jnp.int32, (tt, tt), 1)
    same_chunk_lower = jnp.where((ri >= ci) & (ri // ck == ci // ck), 1.0, 0.0).astype(BF16)
    g1 = g_all.astype(BF16)
    r1 = g_all - g1.astype(F32)
    g2 = r1.astype(BF16)
    g3 = (r1 - g2.astype(F32)).astype(BF16)
    gc_all = _dot(same_chunk_lower, g1) + _dot(same_chunk_lower, g2) + _dot(same_chunk_lower, g3)

    r64 = lax.broadcasted_iota(jnp.int32, (ck, ck), 0)
    c64 = lax.broadcasted_iota(jnp.int32, (ck, ck), 1)
    nt = (((1,), (1,)), ((), ()))
    tn = (((0,), (0,)), ((), ()))

    gcb, egc, kbeta, a_mat, attn, rhs = {}, {}, {}, {}, {}, {}
    raw = {}
    for h, c in chains:
        rows = slice(c * ck, (c + 1) * ck)
        gcb[h, c] = jnp.broadcast_to(gc_all[rows, h:h + 1], (ck, dk))
        betab = jnp.broadcast_to(beta_all[rows, nh + h:nh + h + 1], (ck, dk))
        egc[h, c] = jnp.exp(gcb[h, c])
        kc = kn[h][rows]
        kbeta[h, c] = kc * betab
        rhs[h, c] = jnp.concatenate([head(va, h)[rows] * betab, kbeta[h, c] * egc[h, c]], axis=1)
        lhs = jnp.concatenate([kbeta[h, c], qn[h][rows]], axis=0).astype(BF16)
        raw[h, c] = lax.dot_general(lhs, kc.astype(BF16), nt, preferred_element_type=F32)
    for h, c in chains:
        grow = gcb[h, c].T[:ck, :]
        decay = jnp.where(r64 >= c64, jnp.exp(gcb[h, c][:, :ck] - grow), 0.0)
        a_mat[h, c] = jnp.where(r64 > c64, raw[h, c][:ck] * decay, 0.0)
        attn[h, c] = (raw[h, c][ck:] * decay).astype(BF16)

    def mm3(ah, al, bh, bl):
        return _dot(ah, bh) + _dot(ah, bl) + _dot(al, bh)

    ps = {k: _split2(a_mat[k]) for k in chains}
    ys = {k: _split2(rhs[k]) for k in chains}
    y = {k: rhs[k] - mm3(*ps[k], *ys[k]) for k in chains}
    for _ in range((ck - 1).bit_length() - 1):
        pw = {k: mm3(*ps[k], *ps[k]) for k in chains}
        ps = {k: _split2(pw[k]) for k in chains}
        ys = {k: _split2(y[k]) for k in chains}
        y = {k: y[k] + mm3(*ps[k], *ys[k]) for k in chains}

    state = [s_scr[h] for h in range(nh)]
    for c in range(nc):
        rows = slice(c * ck, (c + 1) * ck)
        s16 = [state[h].astype(BF16) for h in range(nh)]
        v_new = [y[h, c][:, :dk] - _dot(y[h, c][:, dk:].astype(BF16), s16[h]) for h in range(nh)]
        vn16 = [v.astype(BF16) for v in v_new]
        o_c = [_dot((qn[h][rows] * egc[h, c]).astype(BF16), s16[h]) + _dot(attn[h, c], vn16[h])
               for h in range(nh)]
        for h in range(nh):
            g_last = gcb[h, c][ck - 1:ck, :]
            k_dec = kn[h][rows] * jnp.exp(g_last - gcb[h, c])
            state[h] = state[h] * jnp.exp(g_last) + lax.dot_general(
                k_dec.astype(BF16), vn16[h], tn, preferred_element_type=F32)
        for h in range(nh):
            o = o_c[h]
            on = o * lax.rsqrt(jnp.mean(o * o, axis=-1, keepdims=True) + EPS) * ng_ref[...]
            cols = slice(h * dk, (h + 1) * dk)
            o_ref[0, rows, cols] = (on * _silu(z_ref[0, rows, cols])).astype(o_ref.dtype)
    for h in range(nh):
        s_scr[h] = state[h]


def _gated_deltanet(proj3, conv_w, a_log, dt_bias, norm_g):
    b, t, _ = proj3.shape
    nh, dk = GDN_HEADS, GDN_HEAD_DIM
    gw = nh * dk
    tt = min(GDN_ROW_TILE, t)
    col = lambda j: pl.BlockSpec((1, tt, gw), lambda bi, i: (bi, i, j))
    wcol = lambda j: pl.BlockSpec((GDN_CONV, gw), lambda bi, i: (0, j))
    row = pl.BlockSpec((1, LANES), lambda bi, i: (0, 0))
    pad = lambda v: jnp.pad(v.astype(F32), (0, LANES - v.shape[0])).reshape(1, LANES)
    return pl.pallas_call(
        functools.partial(_gdn_kernel, tt=tt),
        out_shape=jax.ShapeDtypeStruct((b, t, gw), BF16),
        grid=(b, t // tt),
        in_specs=[
            col(0), col(1), col(2), col(3),
            pl.BlockSpec((1, tt, LANES), lambda bi, i: (bi, i, 4 * nh)),
            wcol(0), wcol(1), wcol(2),
            row, row, pl.BlockSpec((1, dk), lambda bi, i: (0, 0)),
        ],
        out_specs=pl.BlockSpec((1, tt, gw), lambda bi, i: (bi, i, 0)),
        scratch_shapes=[
            pltpu.VMEM((nh, dk, dk), F32),
            pltpu.VMEM((3, HALO_ROWS, gw), F32),
        ],
        compiler_params=_params("arbitrary", "arbitrary"),
        name="gated_deltanet",
    )(proj3, proj3, proj3, proj3, proj3, conv_w, conv_w, conv_w,
      pad(a_log), pad(dt_bias), norm_g.astype(F32).reshape(1, dk))


def _final_norm_kernel(x_ref, g_ref, o_ref):
    x = x_ref[...]
    o_ref[...] = x * lax.rsqrt(jnp.mean(x * x, axis=-1, keepdims=True) + EPS) * g_ref[...]


def _final_norm(x2, g):
    n, d = x2.shape
    tm = min(ROW_TILE, n)
    return pl.pallas_call(
        _final_norm_kernel,
        out_shape=jax.ShapeDtypeStruct((n, d), F32),
        grid=(n // tm,),
        in_specs=[pl.BlockSpec((tm, d), lambda i: (i, 0)), pl.BlockSpec((1, d), lambda i: (0, 0))],
        out_specs=pl.BlockSpec((tm, d), lambda i: (i, 0)),
        compiler_params=_params("arbitrary"),
        name="final_norm",
    )(x2, g.reshape(1, d))


def _attention_layer(x2, g, mod, w_in, w_out, rope, b, t):
    mw = MOBA_HEADS * HEAD_DIM
    sw = SB_HEADS * HEAD_DIM
    qk, rest = _norm_mod_proj(x2, g, mod, w_in.astype(BF16), t, rope=rope, rope_cols=2 * mw,
                              outs=((0, F32), (2 * mw, BF16)))
    rest3 = rest.reshape(b, t, -1)
    oa = _moba_attention(qk.reshape(b, t, 2 * mw), rest3, 0)
    ob = _sb_attention(rest3, mw, sw)
    o_parts = [oa.reshape(b * t, mw), ob.reshape(b * t, sw)]
    return _proj_residual(o_parts, x2, mod, w_out.astype(BF16), t)


def _gdn_layer(x2, g, mod, w_in, conv_w, a_log, dt_bias, norm_g, w_out, b, t):
    gw = GDN_HEADS * GDN_HEAD_DIM
    w_main = w_in[:, :4 * gw]
    w_ab = jnp.pad(w_in[:, 4 * gw:], ((0, 0), (0, LANES - 2 * GDN_HEADS)))
    w_cat = jnp.concatenate([w_main, w_ab], axis=1).astype(BF16)
    (proj,) = _norm_mod_proj(x2, g, mod, w_cat, t, chunk=384, outs=((0, F32),))
    o = _gated_deltanet(proj.reshape(b, t, -1), conv_w, a_log, dt_bias, norm_g)
    return _proj_residual([o.reshape(b * t, gw)], x2, mod, w_out.astype(BF16), t)


def kernel(x, c, positions, ada_w, ada_b, norm_mix_g, norm_ffn_g, attn_w_in, attn_w_out, gdn_w_in,
           gdn_conv_w, gdn_a_log, gdn_dt_bias, gdn_norm_g, gdn_w_out, ffn_w_in, ffn_w_out,
           final_norm_g):
    b, t, d = x.shape
    depth = ada_w.shape[0]
    assert t % MOBA_BLOCK == 0 and t % GDN_CHUNK == 0
    mods = _modulation(c, ada_w, ada_b)
    rope = _rope_tables(positions)
    x2 = x.reshape(b * t, d)
    f = ffn_w_out.shape[1]
    for layer in range(depth):
        mod = mods[layer].reshape(b, 6, d)
        i = layer // 2
        if layer % 2 == 0:
            x2 = _attention_layer(x2, norm_mix_g[layer], mod, attn_w_in[i], attn_w_out[i], rope, b, t)
        else:
            x2 = _gdn_layer(x2, norm_mix_g[layer], mod, gdn_w_in[i], gdn_conv_w[i], gdn_a_log[i],
                            gdn_dt_bias[i], gdn_norm_g[i], gdn_w_out[i], b, t)
        w_in = ffn_w_in[layer].astype(BF16)
        x2 = _ffn(x2, norm_ffn_g[layer], mod, w_in[:, :f], w_in[:, f:], ffn_w_out[layer].astype(BF16), t)
    return _final_norm(x2, final_norm_g).reshape(b, t, d)
```

```python
import functools

import jax
import jax.numpy as jnp
from jax import lax
from jax.experimental import pallas as pl
from jax.experimental.pallas import tpu as pltpu

F32 = jnp.float32
BF16 = jnp.bfloat16
HIGHEST = lax.Precision.HIGHEST

EPS = 1e-6
HEAD_DIM = 64
MOBA_HEADS = 8
SB_HEADS = 8
MOBA_BLOCK = 256
MOBA_TOPK = 3
ROPE_THETA = 500000.0
ROPE_DIMS = HEAD_DIM // 4
GDN_HEADS = 8
GDN_HEAD_DIM = 128
GDN_CONV = 4
GDN_CHUNK = 64
ATTN_SCALE = HEAD_DIM ** -0.5

LANES = 128
HALO_ROWS = 8
V7X_VMEM_LIMIT = 56 * 1024 * 1024
ROW_TILE = 512
MOD_COL_TILE = 1536
GDN_ROW_TILE = 256
ATTN_HEADS_PER_STEP = 8
HEADS_PER_LANE_GROUP = LANES // HEAD_DIM
SB_UNDERFLOW_LOG = 110.0

SHIFT_M, SCALE_M, GATE_M, SHIFT_F, SCALE_F, GATE_F = range(6)


def _params(*sem):
    return pltpu.CompilerParams(dimension_semantics=sem, vmem_limit_bytes=V7X_VMEM_LIMIT)


def _silu(x):
    return x * jax.nn.sigmoid(x)


def _softplus(x):
    return jnp.maximum(x, 0.0) + jnp.log(1.0 + jnp.exp(-jnp.abs(x)))


def _dot(a, b):
    return jnp.dot(a, b, preferred_element_type=F32)


def _split2(x):
    hi = x.astype(BF16)
    return hi, (x - hi.astype(F32)).astype(BF16)


def _col_chunks(total, width):
    chunks, c0 = [], 0
    while c0 < total:
        cw = min(width, total - c0)
        chunks.append((c0, cw))
        c0 += cw
    return tuple(chunks)


def _mod_kernel(c_ref, w_ref, b_ref, o_ref):
    cond = _silu(c_ref[...])
    o_ref[0] = jnp.dot(cond, w_ref[0], precision=HIGHEST, preferred_element_type=F32) + b_ref[0]


def _modulation(c, ada_w, ada_b):
    depth, d, six_d = ada_w.shape
    b = c.shape[0]
    tn = MOD_COL_TILE
    return pl.pallas_call(
        _mod_kernel,
        out_shape=jax.ShapeDtypeStruct((depth, b, six_d), F32),
        grid=(depth, six_d // tn),
        in_specs=[
            pl.BlockSpec((b, d), lambda l, j: (0, 0)),
            pl.BlockSpec((1, d, tn), lambda l, j: (l, 0, j)),
            pl.BlockSpec((1, 1, tn), lambda l, j: (l, 0, j)),
        ],
        out_specs=pl.BlockSpec((1, b, tn), lambda l, j: (l, 0, j)),
        compiler_params=_params("arbitrary", "arbitrary"),
        name="adaln_mod",
    )(c, ada_w, ada_b.reshape(depth, 1, six_d))


def _rope_kernel(pos_ref, freq_ref, ma_ref, mb_ref, c_ref, sa_ref, sb_ref):
    ang = pos_ref[...] * freq_ref[...]
    s = jnp.sin(ang)
    c_ref[...] = jnp.cos(ang)
    sa_ref[...] = s * ma_ref[...]
    sb_ref[...] = s * mb_ref[...]


def _rope_tables(positions):
    n = positions.size
    half = ROPE_DIMS // 2
    inv_freq = ROPE_THETA ** (-jnp.arange(half, dtype=F32) * 2.0 / ROPE_DIMS)
    lane = jnp.arange(LANES) % HEAD_DIM
    freq = jnp.where(lane < ROPE_DIMS, inv_freq[lane % half], 0.0).astype(F32)[None, :]
    ma = jnp.where(lane < half, -1.0, 0.0).astype(F32)[None, :]
    mb = jnp.where((lane >= half) & (lane < ROPE_DIMS), 1.0, 0.0).astype(F32)[None, :]
    pos = positions.astype(F32).reshape(n, 1)
    tm = min(ROW_TILE, n)
    row = pl.BlockSpec((1, LANES), lambda i: (0, 0))
    tab = pl.BlockSpec((tm, LANES), lambda i: (i, 0))
    return pl.pallas_call(
        _rope_kernel,
        out_shape=[jax.ShapeDtypeStruct((n, LANES), F32)] * 3,
        grid=(n // tm,),
        in_specs=[pl.BlockSpec((tm, 1), lambda i: (i, 0)), row, row, row],
        out_specs=[tab, tab, tab],
        compiler_params=_params("arbitrary"),
        name="rope_tables",
    )(pos, freq, ma, mb)


def _norm_mod(x, g_row, mod_ref, shift_idx, scale_idx):
    y = x * lax.rsqrt(jnp.mean(x * x, axis=-1, keepdims=True) + EPS) * g_row
    return y * (1.0 + mod_ref[0, scale_idx:scale_idx + 1, :]) + mod_ref[0, shift_idx:shift_idx + 1, :]


def _nm_kernel(*refs, chunks, rope_cols, out_cols):
    n_out = len(out_cols)
    x_ref, g_ref, mod_ref, w_ref = refs[:4]
    if rope_cols:
        c_ref, sa_ref, sb_ref = refs[4:7]
    out_refs, h_scr = refs[-1 - n_out:-1], refs[-1]
    h = _norm_mod(x_ref[...], g_ref[...], mod_ref, SHIFT_M, SCALE_M)
    h_scr[...] = h.astype(BF16)
    for c0, cw in chunks:
        o_idx = max(k for k, start in enumerate(out_cols) if start <= c0)
        o_ref, oc0 = out_refs[o_idx], c0 - out_cols[o_idx]
        acc = _dot(h_scr[...], w_ref[:, c0:c0 + cw])
        if c0 < rope_cols:
            reps = cw // LANES
            cos = jnp.tile(c_ref[...], (1, reps))
            sa = jnp.tile(sa_ref[...], (1, reps))
            sb = jnp.tile(sb_ref[...], (1, reps))
            half = ROPE_DIMS // 2
            acc = (acc * cos + pltpu.roll(acc, cw - half, axis=1) * sa
                   + pltpu.roll(acc, half, axis=1) * sb)
        o_ref[:, oc0:oc0 + cw] = acc.astype(o_ref.dtype)


def _norm_mod_proj(x2, g, mod, w, t, *, outs, rope=None, rope_cols=0, chunk=512):
    n, d = x2.shape
    ncol = w.shape[1]
    tm = min(ROW_TILE, t)
    tpb = t // tm
    out_cols = tuple(c0 for c0, _ in outs)
    widths = [end - c0 for c0, end in zip(out_cols, out_cols[1:] + (ncol,))]
    chunks = tuple((out_cols[k] + c0, cw) for k, wd in enumerate(widths) for c0, cw in _col_chunks(wd, chunk))
    assert all(cw % LANES == 0 for _, cw in chunks)
    assert all(c0 + cw <= rope_cols or c0 >= rope_cols for c0, cw in chunks)
    in_specs = [
        pl.BlockSpec((tm, d), lambda i: (i, 0)),
        pl.BlockSpec((1, d), lambda i: (0, 0)),
        pl.BlockSpec((1, 6, d), lambda i: (i // tpb, 0, 0)),
        pl.BlockSpec((d, ncol), lambda i: (0, 0), pipeline_mode=pl.Buffered(1)),
    ]
    args = [x2, g.reshape(1, d), mod, w]
    if rope_cols:
        tab = pl.BlockSpec((tm, LANES), lambda i: (i, 0))
        in_specs += [tab, tab, tab]
        args += list(rope)
    return pl.pallas_call(
        functools.partial(_nm_kernel, chunks=chunks, rope_cols=rope_cols, out_cols=out_cols),
        out_shape=[jax.ShapeDtypeStruct((n, wd), dt) for wd, (_, dt) in zip(widths, outs)],
        grid=(n // tm,),
        in_specs=in_specs,
        out_specs=[pl.BlockSpec((tm, wd), lambda i: (i, 0)) for wd in widths],
        scratch_shapes=[pltpu.VMEM((tm, d), BF16)],
        compiler_params=_params("arbitrary"),
        name="norm_mod_proj",
    )(*args)


def _pair_heads(npair):
    return [(p, a) for p in range(npair) for a in range(HEADS_PER_LANE_GROUP)]


def _pair_cols(p):
    return slice(p * LANES, (p + 1) * LANES)


def _head_rows(a):
    return slice(a * HEAD_DIM, (a + 1) * HEAD_DIM)


def _split_pair_queries(qt2):
    owner = lax.broadcasted_iota(jnp.int32, qt2.shape, 0) // HEAD_DIM
    return [jnp.where(owner == a, qt2, 0.0) for a in range(HEADS_PER_LANE_GROUP)]


def _moba_kernel(q_ref, k_ref, v_ref, o_ref, kb_scr, vt_scr, kmean_scr, sel_scr, *,
                 nblk, blk, n_sel, npair):
    i = pl.program_id(2)
    heads = _pair_heads(npair)

    @pl.when(i == 0)
    def _():
        kmean_scr[...] = jnp.zeros_like(kmean_scr)
        for p in range(npair):
            for j in range(nblk):
                rows = slice(j * blk, (j + 1) * blk)
                kj = k_ref[0, rows, _pair_cols(p)]
                kmean_scr[p, j:j + 1, :] = jnp.mean(kj, axis=0, keepdims=True)
                kb_scr[p, j] = kj.astype(BF16)
                vt_scr[p, j] = v_ref[0, rows, _pair_cols(p)].astype(F32).T.astype(BF16)

    qs = {}
    for p in range(npair):
        for a, qa in enumerate(_split_pair_queries(q_ref[0, :, _pair_cols(p)].T)):
            gate = jnp.dot(kmean_scr[p], qa, precision=HIGHEST, preferred_element_type=F32)
            row = lax.broadcasted_iota(jnp.int32, gate.shape, 0)
            cnt = jnp.zeros(gate.shape, jnp.int32)
            for jp in range(nblk):
                gj = gate[jp:jp + 1, :]
                beats = (gj > gate) | ((gj == gate) & (jp < row))
                cnt = cnt + jnp.where(beats, 1, 0) * (jp < i).astype(jnp.int32)
            sel_scr[p * HEADS_PER_LANE_GROUP + a] = jnp.where((row < i) & (cnt < n_sel), 1.0, 0.0)
            qs[p, a] = (qa * ATTN_SCALE).astype(BF16)

    def values_t(p, a, j):
        return vt_scr[p, j, _head_rows(a), :]

    s = {h: _dot(kb_scr[h[0], i], qs[h]) for h in heads}
    kidx = lax.broadcasted_iota(jnp.int32, (blk, blk), 0)
    qidx = lax.broadcasted_iota(jnp.int32, (blk, blk), 1)
    s = {h: jnp.where(kidx <= qidx, s[h], -jnp.inf) for h in heads}
    m = {h: jnp.max(s[h], axis=0, keepdims=True) for h in heads}
    p_ = {h: jnp.exp(s[h] - m[h]) for h in heads}
    l = {h: jnp.sum(p_[h], axis=0, keepdims=True) for h in heads}
    acc = {h: _dot(values_t(*h, i), p_[h].astype(BF16)) for h in heads}

    def body(j, carry):
        m, l, acc = (dict(zip(heads, c)) for c in carry)
        s = {h: _dot(kb_scr[h[0], j], qs[h]) for h in heads}
        s = {h: jnp.where(sel_scr[h[0] * HEADS_PER_LANE_GROUP + h[1], pl.ds(j, 1), :] > 0.5,
                          s[h], -jnp.inf) for h in heads}
        m_new = {h: jnp.maximum(m[h], jnp.max(s[h], axis=0, keepdims=True)) for h in heads}
        alpha = {h: jnp.exp(m[h] - m_new[h]) for h in heads}
        p_ = {h: jnp.exp(s[h] - m_new[h]) for h in heads}
        l = {h: l[h] * alpha[h] + jnp.sum(p_[h], axis=0, keepdims=True) for h in heads}
        pv = {h: _dot(values_t(*h, j), p_[h].astype(BF16)) for h in heads}
        acc = {h: acc[h] * alpha[h] + pv[h] for h in heads}
        return tuple(tuple(d[h] for h in heads) for d in (m_new, l, acc))

    carry = tuple(tuple(d[h] for h in heads) for d in (m, l, acc))
    m, l, acc = (dict(zip(heads, c)) for c in lax.fori_loop(0, i, body, carry))
    for p in range(npair):
        o_t = jnp.concatenate([acc[p, a] / l[p, a] for a in range(HEADS_PER_LANE_GROUP)], axis=0)
        o_ref[0, :, _pair_cols(p)] = o_t.T.astype(o_ref.dtype)


def _moba_attention(qk3, v3, v_col0):
    b, t, w2 = qk3.shape
    w = w2 // 2
    blk = MOBA_BLOCK
    nblk = t // blk
    hb = ATTN_HEADS_PER_STEP
    npair = hb // HEADS_PER_LANE_GROUP
    gw = hb * HEAD_DIM
    n_sel = min(MOBA_TOPK, max(nblk - 1, 1))
    gate_rows = -(-nblk // 8) * 8
    return pl.pallas_call(
        functools.partial(_moba_kernel, nblk=nblk, blk=blk, n_sel=n_sel, npair=npair),
        out_shape=jax.ShapeDtypeStruct((b, t, w), BF16),
        grid=(b, w // gw, nblk),
        in_specs=[
            pl.BlockSpec((1, blk, gw), lambda bi, hi, i: (bi, i, hi)),
            pl.BlockSpec((1, t, gw), lambda bi, hi, i: (bi, 0, w // gw + hi)),
            pl.BlockSpec((1, t, gw), lambda bi, hi, i: (bi, 0, v_col0 // gw + hi)),
        ],
        out_specs=pl.BlockSpec((1, blk, gw), lambda bi, hi, i: (bi, i, hi)),
        scratch_shapes=[
            pltpu.VMEM((npair, nblk, blk, LANES), BF16),
            pltpu.VMEM((npair, nblk, LANES, blk), BF16),
            pltpu.VMEM((npair, gate_rows, LANES), F32),
            pltpu.VMEM((hb, gate_rows, blk), F32),
        ],
        compiler_params=_params("arbitrary", "arbitrary", "arbitrary"),
        name="moba_attention",
    )(qk3, qk3, v3)


def _sb_kernel(q_ref, k_ref, v_ref, o_ref, vt_scr, *, nblk, blk, npair):
    i = pl.program_id(2)
    heads = _pair_heads(npair)

    @pl.when(i == 0)
    def _():
        for p in range(npair):
            for j in range(nblk):
                rows = slice(j * blk, (j + 1) * blk)
                vt_scr[p, j] = v_ref[0, rows, _pair_cols(p)].astype(F32).T.astype(BF16)

    qs = {}
    for p in range(npair):
        for a, qa in enumerate(_split_pair_queries(q_ref[0, :, _pair_cols(p)].astype(F32).T)):
            qs[p, a] = (qa * ATTN_SCALE).astype(BF16)
    kidx = lax.broadcasted_iota(jnp.int32, (blk, blk), 0)
    qidx = lax.broadcasted_iota(jnp.int32, (blk, blk), 1)
    causal = kidx < qidx
    later = jnp.where(qidx > kidx, 1.0, 0.0).astype(BF16)

    def block(j, tot, acc, diag):
        rows = pl.ds(pl.multiple_of(j * blk, blk), blk)
        z = {h: _dot(k_ref[0, rows, _pair_cols(h[0])], qs[h]) for h in heads}
        sp = {h: _softplus(z[h]) for h in heads}
        spm = {h: jnp.where(causal, sp[h], 0.0) for h in heads} if diag else sp
        parts = {h: _split2(spm[h]) for h in heads}
        suf = {h: _dot(later, parts[h][0]) + _dot(later, parts[h][1]) for h in heads}
        w = {h: jnp.exp(z[h] - sp[h] - suf[h] - tot[h]) for h in heads}
        if diag:
            w = {h: jnp.where(causal, w[h], 0.0) for h in heads}
        pv = {h: _dot(vt_scr[h[0], j, _head_rows(h[1]), :], w[h].astype(BF16)) for h in heads}
        acc = {h: acc[h] + pv[h] for h in heads}
        tot = {h: tot[h] + suf[h][0:1, :] + spm[h][0:1, :] for h in heads}
        return tot, acc

    tot = {h: jnp.zeros((1, blk), F32) for h in heads}
    acc = {h: jnp.zeros((HEAD_DIM, blk), F32) for h in heads}
    tot, acc = block(i, tot, acc, True)

    def pack(tot, acc):
        return tuple(tot[h] for h in heads), tuple(acc[h] for h in heads)

    def alive(carry):
        step, tot, _ = carry
        least = functools.reduce(jnp.minimum, tot)
        return (step < i) & (jnp.min(least) < SB_UNDERFLOW_LOG)

    def body(carry):
        step, tot, acc = carry
        tot, acc = block(i - 1 - step, dict(zip(heads, tot)), dict(zip(heads, acc)), False)
        return (step + 1, *pack(tot, acc))

    _, _, acc = lax.while_loop(alive, body, (jnp.int32(0), *pack(tot, acc)))
    acc = dict(zip(heads, acc))
    for p in range(npair):
        o_t = jnp.concatenate([acc[p, a] for a in range(HEADS_PER_LANE_GROUP)], axis=0)
        o_ref[0, :, _pair_cols(p)] = o_t.T.astype(o_ref.dtype)


def _sb_attention(qkv3, col0, w):
    b, t, _ = qkv3.shape
    blk = MOBA_BLOCK
    nblk = t // blk
    hb = ATTN_HEADS_PER_STEP
    npair = hb // HEADS_PER_LANE_GROUP
    gw = hb * HEAD_DIM
    first = lambda k: (col0 + k * w) // gw
    return pl.pallas_call(
        functools.partial(_sb_kernel, nblk=nblk, blk=blk, npair=npair),
        out_shape=jax.ShapeDtypeStruct((b, t, w), BF16),
        grid=(b, w // gw, nblk),
        in_specs=[
            pl.BlockSpec((1, blk, gw), lambda bi, hi, i: (bi, i, first(0) + hi)),
            pl.BlockSpec((1, t, gw), lambda bi, hi, i: (bi, 0, first(1) + hi)),
            pl.BlockSpec((1, t, gw), lambda bi, hi, i: (bi, 0, first(2) + hi)),
        ],
        out_specs=pl.BlockSpec((1, blk, gw), lambda bi, hi, i: (bi, i, hi)),
        scratch_shapes=[pltpu.VMEM((npair, nblk, LANES, blk), BF16)],
        compiler_params=_params("arbitrary", "arbitrary", "arbitrary"),
        name="sb_attention",
    )(qkv3, qkv3, qkv3)


def _layer_tail_kernel(*refs, splits, chunks, final):
    o_refs = refs[:len(splits)]
    x_ref, mod_ref, w_mix_ref, g_ref, wg_ref, wu_ref, wo_ref = refs[len(splits):len(splits) + 7]
    tail = refs[len(splits) + 7:]
    fg_ref = tail[0] if final else None
    out_ref, h_scr, a_scr = tail[-3:]
    mixed = None
    for o_ref, (r0, r1) in zip(o_refs, splits):
        part = _dot(o_ref[...], w_mix_ref[r0:r1, :])
        mixed = part if mixed is None else mixed + part
    x = x_ref[...] + mod_ref[0, GATE_M:GATE_M + 1, :] * mixed
    h_scr[...] = _norm_mod(x, g_ref[...], mod_ref, SHIFT_F, SCALE_F).astype(BF16)
    for c0, cw in chunks:
        gate = _dot(h_scr[...], wg_ref[:, c0:c0 + cw])
        up = _dot(h_scr[...], wu_ref[:, c0:c0 + cw])
        a_scr[:, c0:c0 + cw] = (_silu(gate) * up).astype(BF16)
    x = x + mod_ref[0, GATE_F:GATE_F + 1, :] * _dot(a_scr[...], wo_ref[...])
    if final:
        x = x * lax.rsqrt(jnp.mean(x * x, axis=-1, keepdims=True) + EPS) * fg_ref[...]
    out_ref[...] = x


def _layer_tail(o_parts, x2, mod, w_mix, g, wg, wu, wo, t, final_g=None):
    n, d = x2.shape
    f = wg.shape[1]
    tm = min(ROW_TILE, t)
    tpb = t // tm
    splits, r0 = [], 0
    for o in o_parts:
        splits.append((r0, r0 + o.shape[1]))
        r0 += o.shape[1]
    assert r0 == w_mix.shape[0]
    final = final_g is not None
    const = lambda shape: pl.BlockSpec(shape, lambda i: (0, 0), pipeline_mode=pl.Buffered(1))
    row = pl.BlockSpec((1, d), lambda i: (0, 0))
    in_specs = [pl.BlockSpec((tm, o.shape[1]), lambda i: (i, 0)) for o in o_parts] + [
        pl.BlockSpec((tm, d), lambda i: (i, 0)),
        pl.BlockSpec((1, 6, d), lambda i: (i // tpb, 0, 0)),
        const(w_mix.shape), row, const((d, f)), const((d, f)), const((f, d)),
    ]
    args = [*o_parts, x2, mod, w_mix, g.reshape(1, d), wg, wu, wo]
    if final:
        in_specs.append(row)
        args.append(final_g.reshape(1, d))
    return pl.pallas_call(
        functools.partial(_layer_tail_kernel, splits=tuple(splits), chunks=_col_chunks(f, 256),
                          final=final),
        out_shape=jax.ShapeDtypeStruct((n, d), F32),
        grid=(n // tm,),
        in_specs=in_specs,
        out_specs=pl.BlockSpec((tm, d), lambda i: (i, 0)),
        scratch_shapes=[pltpu.VMEM((tm, d), BF16), pltpu.VMEM((tm, f), BF16)],
        compiler_params=_params("arbitrary"),
        name="layer_tail",
    )(*args)


def _gdn_kernel(q_ref, k_ref, v_ref, z_ref, ab_ref, wq_ref, wk_ref, wv_ref, alog_ref, dtb_ref,
                ng_ref, o_ref, s_scr, halo_scr, *, tt):
    it = pl.program_id(1)
    ck, dk, nh = GDN_CHUNK, GDN_HEAD_DIM, GDN_HEADS
    nc = tt // ck
    chains = [(h, c) for c in range(nc) for h in range(nh)]

    @pl.when(it == 0)
    def _():
        s_scr[...] = jnp.zeros_like(s_scr)
        halo_scr[...] = jnp.zeros_like(halo_scr)

    def conv_silu(x_ref, w_ref, idx):
        x = x_ref[0]
        prev = halo_scr[idx]
        halo_scr[idx] = x[tt - HALO_ROWS:tt, :]
        row = lax.broadcasted_iota(jnp.int32, prev.shape, 0)
        y = w_ref[GDN_CONV - 1:GDN_CONV, :] * x
        for k in range(1, GDN_CONV):
            rolled = pltpu.roll(x, k, axis=0)
            first = jnp.where(row < k, pltpu.roll(prev, k, axis=0), rolled[:HALO_ROWS])
            shifted = jnp.concatenate([first, rolled[HALO_ROWS:]], axis=0)
            y = y + w_ref[GDN_CONV - 1 - k:GDN_CONV - k, :] * shifted
        return _silu(y)

    qa, ka, va = conv_silu(q_ref, wq_ref, 0), conv_silu(k_ref, wk_ref, 1), conv_silu(v_ref, wv_ref, 2)

    def head(x, h):
        return x[:, h * dk:(h + 1) * dk]

    def l2norm(x):
        return x * lax.rsqrt(jnp.sum(x * x, axis=-1, keepdims=True) + EPS)

    qn = [l2norm(head(qa, h)) * (dk ** -0.5) for h in range(nh)]
    kn = [l2norm(head(ka, h)) for h in range(nh)]

    ab = ab_ref[0]
    g_all = -jnp.exp(alog_ref[...]) * _softplus(ab + dtb_ref[...])
    beta_all = jax.nn.sigmoid(ab)
    ri = lax.broadcasted_iota(jnp.int32, (tt, tt), 0)
    ci = lax.broadcasted_iota(jnp.int32, (tt, tt), 1)
    same_chunk_lower = jnp.where((ri >= ci) & (ri // ck == ci // ck), 1.0, 0.0).astype(BF16)
    g1 = g_all.astype(BF16)
    r1 = g_all - g1.astype(F32)
    g2 = r1.astype(BF16)
    g3 = (r1 - g2.astype(F32)).astype(BF16)
    gc_all = _dot(same_chunk_lower, g1) + _dot(same_chunk_lower, g2) + _dot(same_chunk_lower, g3)

    r64 = lax.broadcasted_iota(jnp.int32, (ck, ck), 0)
    c64 = lax.broadcasted_iota(jnp.int32, (ck, ck), 1)
    nt = (((1,), (1,)), ((), ()))
    tn = (((0,), (0,)), ((), ()))

    gcb, egc, kbeta, a_mat, attn, rhs = {}, {}, {}, {}, {}, {}
    raw = {}
    for h, c in chains:
        rows = slice(c * ck, (c + 1) * ck)
        gcb[h, c] = jnp.broadcast_to(gc_all[rows, h:h + 1], (ck, dk))
        betab = jnp.broadcast_to(beta_all[rows, nh + h:nh + h + 1], (ck, dk))
        egc[h, c] = jnp.exp(gcb[h, c])
        kc = kn[h][rows]
        kbeta[h, c] = kc * betab
        rhs[h, c] = jnp.concatenate([head(va, h)[rows] * betab, kbeta[h, c] * egc[h, c]], axis=1)
        lhs = jnp.concatenate([kbeta[h, c], qn[h][rows]], axis=0).astype(BF16)
        raw[h, c] = lax.dot_general(lhs, kc.astype(BF16), nt, preferred_element_type=F32)
    for h, c in chains:
        grow = gcb[h, c].T[:ck, :]
        decay = jnp.where(r64 >= c64, jnp.exp(gcb[h, c][:, :ck] - grow), 0.0)
        a_mat[h, c] = jnp.where(r64 > c64, raw[h, c][:ck] * decay, 0.0)
        attn[h, c] = (raw[h, c][ck:] * decay).astype(BF16)

    def mm3(ah, al, bh, bl):
        return _dot(ah, bh) + _dot(ah, bl) + _dot(al, bh)

    ps = {k: _split2(a_mat[k]) for k in chains}
    ys = {k: _split2(rhs[k]) for k in chains}
    y = {k: rhs[k] - mm3(*ps[k], *ys[k]) for k in chains}
    for _ in range((ck - 1).bit_length() - 1):
        pw = {k: mm3(*ps[k], *ps[k]) for k in chains}
        ps = {k: _split2(pw[k]) for k in chains}
        ys = {k: _split2(y[k]) for k in chains}
        y = {k: y[k] + mm3(*ps[k], *ys[k]) for k in chains}

    state = [s_scr[h] for h in range(nh)]
    for c in range(nc):
        rows = slice(c * ck, (c + 1) * ck)
        s16 = [state[h].astype(BF16) for h in range(nh)]
        v_new = [y[h, c][:, :dk] - _dot(y[h, c][:, dk:].astype(BF16), s16[h]) for h in range(nh)]
        vn16 = [v.astype(BF16) for v in v_new]
        o_c = [_dot((qn[h][rows] * egc[h, c]).astype(BF16), s16[h]) + _dot(attn[h, c], vn16[h])
               for h in range(nh)]
        for h in range(nh):
            g_last = gcb[h, c][ck - 1:ck, :]
            k_dec = kn[h][rows] * jnp.exp(g_last - gcb[h, c])
            state[h] = state[h] * jnp.exp(g_last) + lax.dot_general(
                k_dec.astype(BF16), vn16[h], tn, preferred_element_type=F32)
        for h in range(nh):
            o = o_c[h]
            on = o * lax.rsqrt(jnp.mean(o * o, axis=-1, keepdims=True) + EPS) * ng_ref[...]
            cols = slice(h * dk, (h + 1) * dk)
            o_ref[0, rows, cols] = (on * _silu(z_ref[0, rows, cols])).astype(o_ref.dtype)
    for h in range(nh):
        s_scr[h] = state[h]


def _gated_deltanet(proj3, conv_w, a_log, dt_bias, norm_g):
    b, t, _ = proj3.shape
    nh, dk = GDN_HEADS, GDN_HEAD_DIM
    gw = nh * dk
    tt = min(GDN_ROW_TILE, t)
    col = lambda j: pl.BlockSpec((1, tt, gw), lambda bi, i: (bi, i, j))
    wcol = lambda j: pl.BlockSpec((GDN_CONV, gw), lambda bi, i: (0, j))
    row = pl.BlockSpec((1, LANES), lambda bi, i: (0, 0))
    pad = lambda v: jnp.pad(v.astype(F32), (0, LANES - v.shape[0])).reshape(1, LANES)
    return pl.pallas_call(
        functools.partial(_gdn_kernel, tt=tt),
        out_shape=jax.ShapeDtypeStruct((b, t, gw), BF16),
        grid=(b, t // tt),
        in_specs=[
            col(0), col(1), col(2), col(3),
            pl.BlockSpec((1, tt, LANES), lambda bi, i: (bi, i, 4 * nh)),
            wcol(0), wcol(1), wcol(2),
            row, row, pl.BlockSpec((1, dk), lambda bi, i: (0, 0)),
        ],
        out_specs=pl.BlockSpec((1, tt, gw), lambda bi, i: (bi, i, 0)),
        scratch_shapes=[
            pltpu.VMEM((nh, dk, dk), F32),
            pltpu.VMEM((3, HALO_ROWS, gw), F32),
        ],
        compiler_params=_params("arbitrary", "arbitrary"),
        name="gated_deltanet",
    )(proj3, proj3, proj3, proj3, proj3, conv_w, conv_w, conv_w,
      pad(a_log), pad(dt_bias), norm_g.astype(F32).reshape(1, dk))


def _attention_mixer(x2, g, mod, w_in, rope, b, t):
    mw = MOBA_HEADS * HEAD_DIM
    sw = SB_HEADS * HEAD_DIM
    qk, rest = _norm_mod_proj(x2, g, mod, w_in.astype(BF16), t, rope=rope, rope_cols=2 * mw,
                              outs=((0, F32), (2 * mw, BF16)))
    rest3 = rest.reshape(b, t, -1)
    oa = _moba_attention(qk.reshape(b, t, 2 * mw), rest3, 0)
    ob = _sb_attention(rest3, mw, sw)
    return [oa.reshape(b * t, mw), ob.reshape(b * t, sw)]


def _gdn_mixer(x2, g, mod, w_in, conv_w, a_log, dt_bias, norm_g, b, t):
    gw = GDN_HEADS * GDN_HEAD_DIM
    w_main = w_in[:, :4 * gw]
    w_ab = jnp.pad(w_in[:, 4 * gw:], ((0, 0), (0, LANES - 2 * GDN_HEADS)))
    w_cat = jnp.concatenate([w_main, w_ab], axis=1).astype(BF16)
    (proj,) = _norm_mod_proj(x2, g, mod, w_cat, t, chunk=384, outs=((0, F32),))
    o = _gated_deltanet(proj.reshape(b, t, -1), conv_w, a_log, dt_bias, norm_g)
    return [o.reshape(b * t, gw)]


def kernel(x, c, positions, ada_w, ada_b, norm_mix_g, norm_ffn_g, attn_w_in, attn_w_out, gdn_w_in,
           gdn_conv_w, gdn_a_log, gdn_dt_bias, gdn_norm_g, gdn_w_out, ffn_w_in, ffn_w_out,
           final_norm_g):
    b, t, d = x.shape
    depth = ada_w.shape[0]
    assert t % MOBA_BLOCK == 0 and t % GDN_CHUNK == 0
    mods = _modulation(c, ada_w, ada_b)
    rope = _rope_tables(positions)
    x2 = x.reshape(b * t, d)
    f = ffn_w_out.shape[1]
    for layer in range(depth):
        mod = mods[layer].reshape(b, 6, d)
        i = layer // 2
        if layer % 2 == 0:
            o_parts = _attention_mixer(x2, norm_mix_g[layer], mod, attn_w_in[i], rope, b, t)
            w_mix = attn_w_out[i]
        else:
            o_parts = _gdn_mixer(x2, norm_mix_g[layer], mod, gdn_w_in[i], gdn_conv_w[i], gdn_a_log[i],
                                 gdn_dt_bias[i], gdn_norm_g[i], b, t)
            w_mix = gdn_w_out[i]
        w_in = ffn_w_in[layer].astype(BF16)
        x2 = _layer_tail(o_parts, x2, mod, w_mix.astype(BF16), norm_ffn_g[layer], w_in[:, :f], w_in[:, f:],
                         ffn_w_out[layer].astype(BF16), t,
                         final_g=final_norm_g if layer == depth - 1 else None)
    return x2.reshape(b, t, d)
```

```python
import functools

import jax
import jax.numpy as jnp
from jax import lax
from jax.experimental import pallas as pl
from jax.experimental.pallas import tpu as pltpu

F32 = jnp.float32
BF16 = jnp.bfloat16
HIGHEST = lax.Precision.HIGHEST

EPS = 1e-6
HEAD_DIM = 64
MOBA_HEADS = 8
SB_HEADS = 8
MOBA_BLOCK = 256
MOBA_TOPK = 3
ROPE_THETA = 500000.0
ROPE_DIMS = HEAD_DIM // 4
GDN_HEADS = 8
GDN_HEAD_DIM = 128
GDN_CONV = 4
GDN_CHUNK = 64
ATTN_SCALE = HEAD_DIM ** -0.5

LANES = 128
HALO_ROWS = 8
V7X_VMEM_LIMIT = 56 * 1024 * 1024
ROW_TILE = 512
MOD_COL_TILE = 1536
GDN_ROW_TILE = 256
ATTN_HEADS_PER_STEP = 8
HEADS_PER_LANE_GROUP = LANES // HEAD_DIM
SB_UNDERFLOW_LOG = 110.0

SHIFT_M, SCALE_M, GATE_M, SHIFT_F, SCALE_F, GATE_F = range(6)


def _params(*sem):
    return pltpu.CompilerParams(dimension_semantics=sem, vmem_limit_bytes=V7X_VMEM_LIMIT)


def _silu(x):
    return x * jax.nn.sigmoid(x)


def _softplus(x):
    return jnp.maximum(x, 0.0) + jnp.log(1.0 + jnp.exp(-jnp.abs(x)))


def _dot(a, b):
    return jnp.dot(a, b, preferred_element_type=F32)


def _split2(x):
    hi = x.astype(BF16)
    return hi, (x - hi.astype(F32)).astype(BF16)


def _col_chunks(total, width):
    chunks, c0 = [], 0
    while c0 < total:
        cw = min(width, total - c0)
        chunks.append((c0, cw))
        c0 += cw
    return tuple(chunks)


def _mod_kernel(c_ref, w_ref, b_ref, o_ref):
    cond = _silu(c_ref[...])
    o_ref[0] = jnp.dot(cond, w_ref[0], precision=HIGHEST, preferred_element_type=F32) + b_ref[0]


def _modulation(c, ada_w, ada_b):
    depth, d, six_d = ada_w.shape
    b = c.shape[0]
    tn = MOD_COL_TILE
    return pl.pallas_call(
        _mod_kernel,
        out_shape=jax.ShapeDtypeStruct((depth, b, six_d), F32),
        grid=(depth, six_d // tn),
        in_specs=[
            pl.BlockSpec((b, d), lambda l, j: (0, 0)),
            pl.BlockSpec((1, d, tn), lambda l, j: (l, 0, j)),
            pl.BlockSpec((1, 1, tn), lambda l, j: (l, 0, j)),
        ],
        out_specs=pl.BlockSpec((1, b, tn), lambda l, j: (l, 0, j)),
        compiler_params=_params("arbitrary", "arbitrary"),
        name="adaln_mod",
    )(c, ada_w, ada_b.reshape(depth, 1, six_d))


def _rope_kernel(pos_ref, freq_ref, ma_ref, mb_ref, c_ref, sa_ref, sb_ref):
    ang = pos_ref[...] * freq_ref[...]
    s = jnp.sin(ang)
    c_ref[...] = jnp.cos(ang)
    sa_ref[...] = s * ma_ref[...]
    sb_ref[...] = s * mb_ref[...]


def _rope_tables(positions):
    n = positions.size
    half = ROPE_DIMS // 2
    inv_freq = ROPE_THETA ** (-jnp.arange(half, dtype=F32) * 2.0 / ROPE_DIMS)
    lane = jnp.arange(LANES) % HEAD_DIM
    freq = jnp.where(lane < ROPE_DIMS, inv_freq[lane % half], 0.0).astype(F32)[None, :]
    ma = jnp.where(lane < half, -1.0, 0.0).astype(F32)[None, :]
    mb = jnp.where((lane >= half) & (lane < ROPE_DIMS), 1.0, 0.0).astype(F32)[None, :]
    pos = positions.astype(F32).reshape(n, 1)
    tm = min(ROW_TILE, n)
    row = pl.BlockSpec((1, LANES), lambda i: (0, 0))
    tab = pl.BlockSpec((tm, LANES), lambda i: (i, 0))
    return pl.pallas_call(
        _rope_kernel,
        out_shape=[jax.ShapeDtypeStruct((n, LANES), F32)] * 3,
        grid=(n // tm,),
        in_specs=[pl.BlockSpec((tm, 1), lambda i: (i, 0)), row, row, row],
        out_specs=[tab, tab, tab],
        compiler_params=_params("arbitrary"),
        name="rope_tables",
    )(pos, freq, ma, mb)


def _norm_mod(x, g_row, mod_ref, shift_idx, scale_idx):
    y = x * lax.rsqrt(jnp.mean(x * x, axis=-1, keepdims=True) + EPS) * g_row
    return y * (1.0 + mod_ref[0, scale_idx:scale_idx + 1, :]) + mod_ref[0, shift_idx:shift_idx + 1, :]


def _causal_conv_silu(x, prev, w_ref, cols):
    row = lax.broadcasted_iota(jnp.int32, prev.shape, 0)
    y = w_ref[GDN_CONV - 1:GDN_CONV, cols] * x
    for k in range(1, GDN_CONV):
        rolled = pltpu.roll(x, k, axis=0)
        first = jnp.where(row < k, pltpu.roll(prev, k, axis=0), rolled[:HALO_ROWS])
        shifted = jnp.concatenate([first, rolled[HALO_ROWS:]], axis=0)
        y = y + w_ref[GDN_CONV - 1 - k:GDN_CONV - k, cols] * shifted
    return _silu(y)


def _nm_kernel(*refs, chunks, rope_cols, out_cols, conv):
    n_out = len(out_cols)
    x_ref, g_ref, mod_ref, w_ref = refs[:4]
    if rope_cols:
        c_ref, sa_ref, sb_ref = refs[4:7]
    if conv:
        cw_ref, halo_scr = refs[4], refs[-1]
        refs = refs[:-1]
        conv_cols, l2_cols, q_cols, tpb = conv
        seq_start = pl.program_id(0) % tpb == 0
    out_refs, h_scr = refs[-1 - n_out:-1], refs[-1]
    h = _norm_mod(x_ref[...], g_ref[...], mod_ref, SHIFT_M, SCALE_M)
    h_scr[...] = h.astype(BF16)
    tm = h.shape[0]
    for c0, cw in chunks:
        o_idx = max(k for k, start in enumerate(out_cols) if start <= c0)
        o_ref, oc0 = out_refs[o_idx], c0 - out_cols[o_idx]
        acc = _dot(h_scr[...], w_ref[:, c0:c0 + cw])
        if conv and c0 < conv_cols:
            cols = slice(c0, c0 + cw)
            prev = jnp.where(seq_start, 0.0, halo_scr[:, cols])
            halo_scr[:, cols] = acc[tm - HALO_ROWS:tm, :]
            acc = _causal_conv_silu(acc, prev, cw_ref, cols)
            groups = []
            for g0 in range(0, cw, GDN_HEAD_DIM):
                seg = acc[:, g0:g0 + GDN_HEAD_DIM]
                if c0 + g0 < l2_cols:
                    seg = seg * lax.rsqrt(jnp.sum(seg * seg, axis=-1, keepdims=True) + EPS)
                if c0 + g0 < q_cols:
                    seg = seg * (GDN_HEAD_DIM ** -0.5)
                groups.append(seg)
            acc = jnp.concatenate(groups, axis=1)
        if c0 < rope_cols:
            reps = cw // LANES
            cos = jnp.tile(c_ref[...], (1, reps))
            sa = jnp.tile(sa_ref[...], (1, reps))
            sb = jnp.tile(sb_ref[...], (1, reps))
            half = ROPE_DIMS // 2
            acc = (acc * cos + pltpu.roll(acc, cw - half, axis=1) * sa
                   + pltpu.roll(acc, half, axis=1) * sb)
        o_ref[:, oc0:oc0 + cw] = acc.astype(o_ref.dtype)


def _norm_mod_proj(x2, g, mod, w, t, *, outs, rope=None, rope_cols=0, conv_w=None, conv_l2_cols=0,
                   conv_q_cols=0, chunk=512):
    n, d = x2.shape
    ncol = w.shape[1]
    tm = min(ROW_TILE, t)
    tpb = t // tm
    out_cols = tuple(c0 for c0, _ in outs)
    widths = [end - c0 for c0, end in zip(out_cols, out_cols[1:] + (ncol,))]
    chunks = tuple((out_cols[k] + c0, cw) for k, wd in enumerate(widths) for c0, cw in _col_chunks(wd, chunk))
    assert all(cw % LANES == 0 for _, cw in chunks)
    assert all(c0 + cw <= rope_cols or c0 >= rope_cols for c0, cw in chunks)
    in_specs = [
        pl.BlockSpec((tm, d), lambda i: (i, 0)),
        pl.BlockSpec((1, d), lambda i: (0, 0)),
        pl.BlockSpec((1, 6, d), lambda i: (i // tpb, 0, 0)),
        pl.BlockSpec((d, ncol), lambda i: (0, 0), pipeline_mode=pl.Buffered(1)),
    ]
    args = [x2, g.reshape(1, d), mod, w]
    if rope_cols:
        tab = pl.BlockSpec((tm, LANES), lambda i: (i, 0))
        in_specs += [tab, tab, tab]
        args += list(rope)
    scratch = [pltpu.VMEM((tm, d), BF16)]
    conv = None
    if conv_w is not None:
        conv_cols = conv_w.shape[1]
        assert all(c0 + cw <= conv_cols or c0 >= conv_cols for c0, cw in chunks)
        conv = (conv_cols, conv_l2_cols, conv_q_cols, tpb)
        in_specs.append(pl.BlockSpec(conv_w.shape, lambda i: (0, 0)))
        args.append(conv_w)
        scratch.append(pltpu.VMEM((HALO_ROWS, conv_cols), F32))
    return pl.pallas_call(
        functools.partial(_nm_kernel, chunks=chunks, rope_cols=rope_cols, out_cols=out_cols, conv=conv),
        out_shape=[jax.ShapeDtypeStruct((n, wd), dt) for wd, (_, dt) in zip(widths, outs)],
        grid=(n // tm,),
        in_specs=in_specs,
        out_specs=[pl.BlockSpec((tm, wd), lambda i: (i, 0)) for wd in widths],
        scratch_shapes=scratch,
        compiler_params=_params("arbitrary"),
        name="norm_mod_proj",
    )(*args)


def _pair_heads(npair):
    return [(p, a) for p in range(npair) for a in range(HEADS_PER_LANE_GROUP)]


def _pair_cols(p):
    return slice(p * LANES, (p + 1) * LANES)


def _head_rows(a):
    return slice(a * HEAD_DIM, (a + 1) * HEAD_DIM)


def _split_pair_queries(qt2):
    owner = lax.broadcasted_iota(jnp.int32, qt2.shape, 0) // HEAD_DIM
    return [jnp.where(owner == a, qt2, 0.0) for a in range(HEADS_PER_LANE_GROUP)]


def _moba_kernel(q_ref, k_ref, v_ref, o_ref, kb_scr, vt_scr, kmean_scr, sel_scr, *,
                 nblk, blk, n_sel, npair):
    i = pl.program_id(2)
    heads = _pair_heads(npair)

    @pl.when(i == 0)
    def _():
        kmean_scr[...] = jnp.zeros_like(kmean_scr)
        for p in range(npair):
            for j in range(nblk):
                rows = slice(j * blk, (j + 1) * blk)
                kj = k_ref[0, rows, _pair_cols(p)]
                kmean_scr[p, j:j + 1, :] = jnp.mean(kj, axis=0, keepdims=True)
                kb_scr[p, j] = kj.astype(BF16)
                vt_scr[p, j] = v_ref[0, rows, _pair_cols(p)].astype(F32).T.astype(BF16)

    qs = {}
    for p in range(npair):
        for a, qa in enumerate(_split_pair_queries(q_ref[0, :, _pair_cols(p)].T)):
            gate = jnp.dot(kmean_scr[p], qa, precision=HIGHEST, preferred_element_type=F32)
            row = lax.broadcasted_iota(jnp.int32, gate.shape, 0)
            cnt = jnp.zeros(gate.shape, jnp.int32)
            for jp in range(nblk):
                gj = gate[jp:jp + 1, :]
                beats = (gj > gate) | ((gj == gate) & (jp < row))
                cnt = cnt + jnp.where(beats, 1, 0) * (jp < i).astype(jnp.int32)
            sel_scr[p * HEADS_PER_LANE_GROUP + a] = jnp.where((row < i) & (cnt < n_sel), 1.0, 0.0)
            qs[p, a] = (qa * ATTN_SCALE).astype(BF16)

    def values_t(p, a, j):
        return vt_scr[p, j, _head_rows(a), :]

    s = {h: _dot(kb_scr[h[0], i], qs[h]) for h in heads}
    kidx = lax.broadcasted_iota(jnp.int32, (blk, blk), 0)
    qidx = lax.broadcasted_iota(jnp.int32, (blk, blk), 1)
    s = {h: jnp.where(kidx <= qidx, s[h], -jnp.inf) for h in heads}
    m = {h: jnp.max(s[h], axis=0, keepdims=True) for h in heads}
    p_ = {h: jnp.exp(s[h] - m[h]) for h in heads}
    l = {h: jnp.sum(p_[h], axis=0, keepdims=True) for h in heads}
    acc = {h: _dot(values_t(*h, i), p_[h].astype(BF16)) for h in heads}

    def body(j, carry):
        m, l, acc = (dict(zip(heads, c)) for c in carry)
        s = {h: _dot(kb_scr[h[0], j], qs[h]) for h in heads}
        s = {h: jnp.where(sel_scr[h[0] * HEADS_PER_LANE_GROUP + h[1], pl.ds(j, 1), :] > 0.5,
                          s[h], -jnp.inf) for h in heads}
        m_new = {h: jnp.maximum(m[h], jnp.max(s[h], axis=0, keepdims=True)) for h in heads}
        alpha = {h: jnp.exp(m[h] - m_new[h]) for h in heads}
        p_ = {h: jnp.exp(s[h] - m_new[h]) for h in heads}
        l = {h: l[h] * alpha[h] + jnp.sum(p_[h], axis=0, keepdims=True) for h in heads}
        pv = {h: _dot(values_t(*h, j), p_[h].astype(BF16)) for h in heads}
        acc = {h: acc[h] * alpha[h] + pv[h] for h in heads}
        return tuple(tuple(d[h] for h in heads) for d in (m_new, l, acc))

    carry = tuple(tuple(d[h] for h in heads) for d in (m, l, acc))
    m, l, acc = (dict(zip(heads, c)) for c in lax.fori_loop(0, i, body, carry))
    for p in range(npair):
        o_t = jnp.concatenate([acc[p, a] / l[p, a] for a in range(HEADS_PER_LANE_GROUP)], axis=0)
        o_ref[0, :, _pair_cols(p)] = o_t.T.astype(o_ref.dtype)


def _moba_attention(qk3, v3, v_col0):
    b, t, w2 = qk3.shape
    w = w2 // 2
    blk = MOBA_BLOCK
    nblk = t // blk
    hb = ATTN_HEADS_PER_STEP
    npair = hb // HEADS_PER_LANE_GROUP
    gw = hb * HEAD_DIM
    n_sel = min(MOBA_TOPK, max(nblk - 1, 1))
    gate_rows = -(-nblk // 8) * 8
    return pl.pallas_call(
        functools.partial(_moba_kernel, nblk=nblk, blk=blk, n_sel=n_sel, npair=npair),
        out_shape=jax.ShapeDtypeStruct((b, t, w), BF16),
        grid=(b, w // gw, nblk),
        in_specs=[
            pl.BlockSpec((1, blk, gw), lambda bi, hi, i: (bi, i, hi)),
            pl.BlockSpec((1, t, gw), lambda bi, hi, i: (bi, 0, w // gw + hi)),
            pl.BlockSpec((1, t, gw), lambda bi, hi, i: (bi, 0, v_col0 // gw + hi)),
        ],
        out_specs=pl.BlockSpec((1, blk, gw), lambda bi, hi, i: (bi, i, hi)),
        scratch_shapes=[
            pltpu.VMEM((npair, nblk, blk, LANES), BF16),
            pltpu.VMEM((npair, nblk, LANES, blk), BF16),
            pltpu.VMEM((npair, gate_rows, LANES), F32),
            pltpu.VMEM((hb, gate_rows, blk), F32),
        ],
        compiler_params=_params("arbitrary", "arbitrary", "arbitrary"),
        name="moba_attention",
    )(qk3, qk3, v3)


def _sb_kernel(q_ref, k_ref, v_ref, o_ref, vt_scr, *, nblk, blk, npair):
    i = pl.program_id(2)
    heads = _pair_heads(npair)

    @pl.when(i == 0)
    def _():
        for p in range(npair):
            for j in range(nblk):
                rows = slice(j * blk, (j + 1) * blk)
                vt_scr[p, j] = v_ref[0, rows, _pair_cols(p)].astype(F32).T.astype(BF16)

    qs = {}
    for p in range(npair):
        for a, qa in enumerate(_split_pair_queries(q_ref[0, :, _pair_cols(p)].astype(F32).T)):
            qs[p, a] = (qa * ATTN_SCALE).astype(BF16)
    kidx = lax.broadcasted_iota(jnp.int32, (blk, blk), 0)
    qidx = lax.broadcasted_iota(jnp.int32, (blk, blk), 1)
    causal = kidx < qidx
    later = jnp.where(qidx > kidx, 1.0, 0.0).astype(BF16)

    def block(j, tot, acc, diag):
        rows = pl.ds(pl.multiple_of(j * blk, blk), blk)
        z = {h: _dot(k_ref[0, rows, _pair_cols(h[0])], qs[h]) for h in heads}
        sp = {h: _softplus(z[h]) for h in heads}
        spm = {h: jnp.where(causal, sp[h], 0.0) for h in heads} if diag else sp
        parts = {h: _split2(spm[h]) for h in heads}
        suf = {h: _dot(later, parts[h][0]) + _dot(later, parts[h][1]) for h in heads}
        w = {h: jnp.exp(z[h] - sp[h] - suf[h] - tot[h]) for h in heads}
        if diag:
            w = {h: jnp.where(causal, w[h], 0.0) for h in heads}
        pv = {h: _dot(vt_scr[h[0], j, _head_rows(h[1]), :], w[h].astype(BF16)) for h in heads}
        acc = {h: acc[h] + pv[h] for h in heads}
        tot = {h: tot[h] + suf[h][0:1, :] + spm[h][0:1, :] for h in heads}
        return tot, acc

    tot = {h: jnp.zeros((1, blk), F32) for h in heads}
    acc = {h: jnp.zeros((HEAD_DIM, blk), F32) for h in heads}
    tot, acc = block(i, tot, acc, True)

    def pack(tot, acc):
        return tuple(tot[h] for h in heads), tuple(acc[h] for h in heads)

    def alive(carry):
        step, tot, _ = carry
        least = functools.reduce(jnp.minimum, tot)
        return (step < i) & (jnp.min(least) < SB_UNDERFLOW_LOG)

    def body(carry):
        step, tot, acc = carry
        tot, acc = block(i - 1 - step, dict(zip(heads, tot)), dict(zip(heads, acc)), False)
        return (step + 1, *pack(tot, acc))

    _, _, acc = lax.while_loop(alive, body, (jnp.int32(0), *pack(tot, acc)))
    acc = dict(zip(heads, acc))
    for p in range(npair):
        o_t = jnp.concatenate([acc[p, a] for a in range(HEADS_PER_LANE_GROUP)], axis=0)
        o_ref[0, :, _pair_cols(p)] = o_t.T.astype(o_ref.dtype)


def _sb_attention(qkv3, col0, w):
    b, t, _ = qkv3.shape
    blk = MOBA_BLOCK
    nblk = t // blk
    hb = ATTN_HEADS_PER_STEP
    npair = hb // HEADS_PER_LANE_GROUP
    gw = hb * HEAD_DIM
    first = lambda k: (col0 + k * w) // gw
    return pl.pallas_call(
        functools.partial(_sb_kernel, nblk=nblk, blk=blk, npair=npair),
        out_shape=jax.ShapeDtypeStruct((b, t, w), BF16),
        grid=(b, w // gw, nblk),
        in_specs=[
            pl.BlockSpec((1, blk, gw), lambda bi, hi, i: (bi, i, first(0) + hi)),
            pl.BlockSpec((1, t, gw), lambda bi, hi, i: (bi, 0, first(1) + hi)),
            pl.BlockSpec((1, t, gw), lambda bi, hi, i: (bi, 0, first(2) + hi)),
        ],
        out_specs=pl.BlockSpec((1, blk, gw), lambda bi, hi, i: (bi, i, hi)),
        scratch_shapes=[pltpu.VMEM((npair, nblk, LANES, blk), BF16)],
        compiler_params=_params("arbitrary", "arbitrary", "arbitrary"),
        name="sb_attention",
    )(qkv3, qkv3, qkv3)


def _layer_tail_kernel(*refs, splits, chunks, final):
    o_refs = refs[:len(splits)]
    x_ref, mod_ref, w_mix_ref, g_ref, wg_ref, wu_ref, wo_ref = refs[len(splits):len(splits) + 7]
    tail = refs[len(splits) + 7:]
    fg_ref = tail[0] if final else None
    out_ref, h_scr, a_scr = tail[-3:]
    mixed = None
    for o_ref, (r0, r1) in zip(o_refs, splits):
        part = _dot(o_ref[...], w_mix_ref[r0:r1, :])
        mixed = part if mixed is None else mixed + part
    x = x_ref[...] + mod_ref[0, GATE_M:GATE_M + 1, :] * mixed
    h_scr[...] = _norm_mod(x, g_ref[...], mod_ref, SHIFT_F, SCALE_F).astype(BF16)
    for c0, cw in chunks:
        gate = _dot(h_scr[...], wg_ref[:, c0:c0 + cw])
        up = _dot(h_scr[...], wu_ref[:, c0:c0 + cw])
        a_scr[:, c0:c0 + cw] = (_silu(gate) * up).astype(BF16)
    x = x + mod_ref[0, GATE_F:GATE_F + 1, :] * _dot(a_scr[...], wo_ref[...])
    if final:
        x = x * lax.rsqrt(jnp.mean(x * x, axis=-1, keepdims=True) + EPS) * fg_ref[...]
    out_ref[...] = x


def _layer_tail(o_parts, x2, mod, w_mix, g, wg, wu, wo, t, final_g=None):
    n, d = x2.shape
    f = wg.shape[1]
    tm = min(ROW_TILE, t)
    tpb = t // tm
    splits, r0 = [], 0
    for o in o_parts:
        splits.append((r0, r0 + o.shape[1]))
        r0 += o.shape[1]
    assert r0 == w_mix.shape[0]
    final = final_g is not None
    const = lambda shape: pl.BlockSpec(shape, lambda i: (0, 0), pipeline_mode=pl.Buffered(1))
    row = pl.BlockSpec((1, d), lambda i: (0, 0))
    in_specs = [pl.BlockSpec((tm, o.shape[1]), lambda i: (i, 0)) for o in o_parts] + [
        pl.BlockSpec((tm, d), lambda i: (i, 0)),
        pl.BlockSpec((1, 6, d), lambda i: (i // tpb, 0, 0)),
        const(w_mix.shape), row, const((d, f)), const((d, f)), const((f, d)),
    ]
    args = [*o_parts, x2, mod, w_mix, g.reshape(1, d), wg, wu, wo]
    if final:
        in_specs.append(row)
        args.append(final_g.reshape(1, d))
    return pl.pallas_call(
        functools.partial(_layer_tail_kernel, splits=tuple(splits), chunks=_col_chunks(f, 256),
                          final=final),
        out_shape=jax.ShapeDtypeStruct((n, d), F32),
        grid=(n // tm,),
        in_specs=in_specs,
        out_specs=pl.BlockSpec((tm, d), lambda i: (i, 0)),
        scratch_shapes=[pltpu.VMEM((tm, d), BF16), pltpu.VMEM((tm, f), BF16)],
        compiler_params=_params("arbitrary"),
        name="layer_tail",
    )(*args)


def _gdn_kernel(q_ref, k_ref, v_ref, z_ref, ab_ref, alog_ref, dtb_ref, ng_ref, o_ref, s_scr, *, tt):
    it = pl.program_id(1)
    ck, dk, nh = GDN_CHUNK, GDN_HEAD_DIM, GDN_HEADS
    nc = tt // ck
    chains = [(h, c) for c in range(nc) for h in range(nh)]

    @pl.when(it == 0)
    def _():
        s_scr[...] = jnp.zeros_like(s_scr)

    def head(x, h):
        return x[:, h * dk:(h + 1) * dk]

    va = v_ref[0]
    qn = [head(q_ref[0], h) for h in range(nh)]
    kn = [head(k_ref[0], h) for h in range(nh)]

    ab = ab_ref[0]
    g_all = -jnp.exp(alog_ref[...]) * _softplus(ab + dtb_ref[...])
    beta_all = jax.nn.sigmoid(ab)
    ri = lax.broadcasted_iota(jnp.int32, (tt, tt), 0)
    ci = lax.broadcasted_iota(jnp.int32, (tt, tt), 1)
    same_chunk_lower = jnp.where((ri >= ci) & (ri // ck == ci // ck), 1.0, 0.0).astype(BF16)
    g1 = g_all.astype(BF16)
    r1 = g_all - g1.astype(F32)
    g2 = r1.astype(BF16)
    g3 = (r1 - g2.astype(F32)).astype(BF16)
    gc_all = _dot(same_chunk_lower, g1) + _dot(same_chunk_lower, g2) + _dot(same_chunk_lower, g3)

    r64 = lax.broadcasted_iota(jnp.int32, (ck, ck), 0)
    c64 = lax.broadcasted_iota(jnp.int32, (ck, ck), 1)
    nt = (((1,), (1,)), ((), ()))
    tn = (((0,), (0,)), ((), ()))

    gcb, egc, kbeta, a_mat, attn, rhs = {}, {}, {}, {}, {}, {}
    raw = {}
    for h, c in chains:
        rows = slice(c * ck, (c + 1) * ck)
        gcb[h, c] = jnp.broadcast_to(gc_all[rows, h:h + 1], (ck, dk))
        betab = jnp.broadcast_to(beta_all[rows, nh + h:nh + h + 1], (ck, dk))
        egc[h, c] = jnp.exp(gcb[h, c])
        kc = kn[h][rows]
        kbeta[h, c] = kc * betab
        rhs[h, c] = jnp.concatenate([head(va, h)[rows] * betab, kbeta[h, c] * egc[h, c]], axis=1)
        lhs = jnp.concatenate([kbeta[h, c], qn[h][rows]], axis=0).astype(BF16)
        raw[h, c] = lax.dot_general(lhs, kc.astype(BF16), nt, preferred_element_type=F32)
    for h, c in chains:
        grow = gcb[h, c].T[:ck, :]
        decay = jnp.where(r64 >= c64, jnp.exp(gcb[h, c][:, :ck] - grow), 0.0)
        a_mat[h, c] = jnp.where(r64 > c64, raw[h, c][:ck] * decay, 0.0)
        attn[h, c] = (raw[h, c][ck:] * decay).astype(BF16)

    def mm3(ah, al, bh, bl):
        return _dot(ah, bh) + _dot(ah, bl) + _dot(al, bh)

    ps = {k: _split2(a_mat[k]) for k in chains}
    ys = {k: _split2(rhs[k]) for k in chains}
    y = {k: rhs[k] - mm3(*ps[k], *ys[k]) for k in chains}
    for _ in range((ck - 1).bit_length() - 1):
        pw = {k: mm3(*ps[k], *ps[k]) for k in chains}
        ps = {k: _split2(pw[k]) for k in chains}
        ys = {k: _split2(y[k]) for k in chains}
        y = {k: y[k] + mm3(*ps[k], *ys[k]) for k in chains}

    state = [s_scr[h] for h in range(nh)]
    for c in range(nc):
        rows = slice(c * ck, (c + 1) * ck)
        s16 = [state[h].astype(BF16) for h in range(nh)]
        v_new = [y[h, c][:, :dk] - _dot(y[h, c][:, dk:].astype(BF16), s16[h]) for h in range(nh)]
        vn16 = [v.astype(BF16) for v in v_new]
        o_c = [_dot((qn[h][rows] * egc[h, c]).astype(BF16), s16[h]) + _dot(attn[h, c], vn16[h])
               for h in range(nh)]
        for h in range(nh):
            g_last = gcb[h, c][ck - 1:ck, :]
            k_dec = kn[h][rows] * jnp.exp(g_last - gcb[h, c])
            state[h] = state[h] * jnp.exp(g_last) + lax.dot_general(
                k_dec.astype(BF16), vn16[h], tn, preferred_element_type=F32)
        for h in range(nh):
            o = o_c[h]
            on = o * lax.rsqrt(jnp.mean(o * o, axis=-1, keepdims=True) + EPS) * ng_ref[...]
            cols = slice(h * dk, (h + 1) * dk)
            o_ref[0, rows, cols] = (on * _silu(z_ref[0, rows, cols])).astype(o_ref.dtype)
    for h in range(nh):
        s_scr[h] = state[h]


def _gated_deltanet(proj3, a_log, dt_bias, norm_g):
    b, t, _ = proj3.shape
    nh, dk = GDN_HEADS, GDN_HEAD_DIM
    gw = nh * dk
    tt = min(GDN_ROW_TILE, t)
    col = lambda j: pl.BlockSpec((1, tt, gw), lambda bi, i: (bi, i, j))
    row = pl.BlockSpec((1, LANES), lambda bi, i: (0, 0))
    pad = lambda v: jnp.pad(v.astype(F32), (0, LANES - v.shape[0])).reshape(1, LANES)
    return pl.pallas_call(
        functools.partial(_gdn_kernel, tt=tt),
        out_shape=jax.ShapeDtypeStruct((b, t, gw), BF16),
        grid=(b, t // tt),
        in_specs=[
            col(0), col(1), col(2), col(3),
            pl.BlockSpec((1, tt, LANES), lambda bi, i: (bi, i, 4 * nh)),
            row, row, pl.BlockSpec((1, dk), lambda bi, i: (0, 0)),
        ],
        out_specs=pl.BlockSpec((1, tt, gw), lambda bi, i: (bi, i, 0)),
        scratch_shapes=[pltpu.VMEM((nh, dk, dk), F32)],
        compiler_params=_params("arbitrary", "arbitrary"),
        name="gated_deltanet",
    )(proj3, proj3, proj3, proj3, proj3, pad(a_log), pad(dt_bias), norm_g.astype(F32).reshape(1, dk))


def _attention_mixer(x2, g, mod, w_in, rope, b, t):
    mw = MOBA_HEADS * HEAD_DIM
    sw = SB_HEADS * HEAD_DIM
    qk, rest = _norm_mod_proj(x2, g, mod, w_in.astype(BF16), t, rope=rope, rope_cols=2 * mw,
                              outs=((0, F32), (2 * mw, BF16)))
    rest3 = rest.reshape(b, t, -1)
    oa = _moba_attention(qk.reshape(b, t, 2 * mw), rest3, 0)
    ob = _sb_attention(rest3, mw, sw)
    return [oa.reshape(b * t, mw), ob.reshape(b * t, sw)]


def _gdn_mixer(x2, g, mod, w_in, conv_w, a_log, dt_bias, norm_g, b, t):
    gw = GDN_HEADS * GDN_HEAD_DIM
    w_main = w_in[:, :4 * gw]
    w_ab = jnp.pad(w_in[:, 4 * gw:], ((0, 0), (0, LANES - 2 * GDN_HEADS)))
    w_cat = jnp.concatenate([w_main, w_ab], axis=1).astype(BF16)
    (proj,) = _norm_mod_proj(x2, g, mod, w_cat, t, chunk=384, outs=((0, F32),), conv_w=conv_w,
                             conv_l2_cols=2 * gw, conv_q_cols=gw)
    o = _gated_deltanet(proj.reshape(b, t, -1), a_log, dt_bias, norm_g)
    return [o.reshape(b * t, gw)]


def kernel(x, c, positions, ada_w, ada_b, norm_mix_g, norm_ffn_g, attn_w_in, attn_w_out, gdn_w_in,
           gdn_conv_w, gdn_a_log, gdn_dt_bias, gdn_norm_g, gdn_w_out, ffn_w_in, ffn_w_out,
           final_norm_g):
    b, t, d = x.shape
    depth = ada_w.shape[0]
    assert t % MOBA_BLOCK == 0 and t % GDN_CHUNK == 0
    mods = _modulation(c, ada_w, ada_b)
    rope = _rope_tables(positions)
    x2 = x.reshape(b * t, d)
    f = ffn_w_out.shape[1]
    for layer in range(depth):
        mod = mods[layer].reshape(b, 6, d)
        i = layer // 2
        if layer % 2 == 0:
            o_parts = _attention_mixer(x2, norm_mix_g[layer], mod, attn_w_in[i], rope, b, t)
            w_mix = attn_w_out[i]
        else:
            o_parts = _gdn_mixer(x2, norm_mix_g[layer], mod, gdn_w_in[i], gdn_conv_w[i], gdn_a_log[i],
                                 gdn_dt_bias[i], gdn_norm_g[i], b, t)
            w_mix = gdn_w_out[i]
        w_in = ffn_w_in[layer].astype(BF16)
        x2 = _layer_tail(o_parts, x2, mod, w_mix.astype(BF16), norm_ffn_g[layer], w_in[:, :f], w_in[:, f:],
                         ffn_w_out[layer].astype(BF16), t,
                         final_g=final_norm_g if layer == depth - 1 else None)
    return x2.reshape(b, t, d)
```

```python
import functools

import jax
import jax.numpy as jnp
from jax import lax
from jax.experimental import pallas as pl
from jax.experimental.pallas import tpu as pltpu

F32 = jnp.float32
BF16 = jnp.bfloat16
HIGHEST = lax.Precision.HIGHEST

EPS = 1e-6
HEAD_DIM = 64
MOBA_HEADS = 8
SB_HEADS = 8
MOBA_BLOCK = 256
MOBA_TOPK = 3
ROPE_THETA = 500000.0
ROPE_DIMS = HEAD_DIM // 4
GDN_HEADS = 8
GDN_HEAD_DIM = 128
GDN_CONV = 4
GDN_CHUNK = 64
ATTN_SCALE = HEAD_DIM ** -0.5

LANES = 128
HALO_ROWS = 8
V7X_VMEM_LIMIT = 56 * 1024 * 1024
ROW_TILE = 512
MOD_COL_TILE = 1536
GDN_ROW_TILE = 256
ATTN_HEADS_PER_STEP = 8
HEADS_PER_LANE_GROUP = LANES // HEAD_DIM
SB_UNDERFLOW_LOG = 110.0

SHIFT_M, SCALE_M, GATE_M, SHIFT_F, SCALE_F, GATE_F = range(6)


def _params(*sem):
    return pltpu.CompilerParams(dimension_semantics=sem, vmem_limit_bytes=V7X_VMEM_LIMIT)


def _silu(x):
    return x * jax.nn.sigmoid(x)


def _softplus(x):
    return jnp.maximum(x, 0.0) + jnp.log(1.0 + jnp.exp(-jnp.abs(x)))


def _dot(a, b):
    return jnp.dot(a, b, preferred_element_type=F32)


def _split2(x):
    hi = x.astype(BF16)
    return hi, (x - hi.astype(F32)).astype(BF16)


def _col_chunks(total, width):
    chunks, c0 = [], 0
    while c0 < total:
        cw = min(width, total - c0)
        chunks.append((c0, cw))
        c0 += cw
    return tuple(chunks)


def _mod_kernel(c_ref, w_ref, b_ref, o_ref):
    cond = _silu(c_ref[...])
    o_ref[0] = jnp.dot(cond, w_ref[0], precision=HIGHEST, preferred_element_type=F32) + b_ref[0]


def _modulation(c, ada_w, ada_b):
    depth, d, six_d = ada_w.shape
    b = c.shape[0]
    tn = MOD_COL_TILE
    return pl.pallas_call(
        _mod_kernel,
        out_shape=jax.ShapeDtypeStruct((depth, b, six_d), F32),
        grid=(depth, six_d // tn),
        in_specs=[
            pl.BlockSpec((b, d), lambda l, j: (0, 0)),
            pl.BlockSpec((1, d, tn), lambda l, j: (l, 0, j)),
            pl.BlockSpec((1, 1, tn), lambda l, j: (l, 0, j)),
        ],
        out_specs=pl.BlockSpec((1, b, tn), lambda l, j: (l, 0, j)),
        compiler_params=_params("arbitrary", "arbitrary"),
        name="adaln_mod",
    )(c, ada_w, ada_b.reshape(depth, 1, six_d))


def _rope_kernel(pos_ref, freq_ref, ma_ref, mb_ref, c_ref, sa_ref, sb_ref):
    ang = pos_ref[...] * freq_ref[...]
    s = jnp.sin(ang)
    c_ref[...] = jnp.cos(ang)
    sa_ref[...] = s * ma_ref[...]
    sb_ref[...] = s * mb_ref[...]


def _rope_tables(positions):
    n = positions.size
    half = ROPE_DIMS // 2
    inv_freq = ROPE_THETA ** (-jnp.arange(half, dtype=F32) * 2.0 / ROPE_DIMS)
    lane = jnp.arange(LANES) % HEAD_DIM
    freq = jnp.where(lane < ROPE_DIMS, inv_freq[lane % half], 0.0).astype(F32)[None, :]
    ma = jnp.where(lane < half, -1.0, 0.0).astype(F32)[None, :]
    mb = jnp.where((lane >= half) & (lane < ROPE_DIMS), 1.0, 0.0).astype(F32)[None, :]
    pos = positions.astype(F32).reshape(n, 1)
    tm = min(ROW_TILE, n)
    row = pl.BlockSpec((1, LANES), lambda i: (0, 0))
    tab = pl.BlockSpec((tm, LANES), lambda i: (i, 0))
    return pl.pallas_call(
        _rope_kernel,
        out_shape=[jax.ShapeDtypeStruct((n, LANES), F32)] * 3,
        grid=(n // tm,),
        in_specs=[pl.BlockSpec((tm, 1), lambda i: (i, 0)), row, row, row],
        out_specs=[tab, tab, tab],
        compiler_params=_params("arbitrary"),
        name="rope_tables",
    )(pos, freq, ma, mb)


def _norm_mod(x, g_row, mod_ref, shift_idx, scale_idx):
    y = x * lax.rsqrt(jnp.mean(x * x, axis=-1, keepdims=True) + EPS) * g_row
    return y * (1.0 + mod_ref[0, scale_idx:scale_idx + 1, :]) + mod_ref[0, shift_idx:shift_idx + 1, :]


def _causal_conv_silu(x, prev, w_ref, cols):
    row = lax.broadcasted_iota(jnp.int32, prev.shape, 0)
    y = w_ref[GDN_CONV - 1:GDN_CONV, cols] * x
    for k in range(1, GDN_CONV):
        rolled = pltpu.roll(x, k, axis=0)
        first = jnp.where(row < k, pltpu.roll(prev, k, axis=0), rolled[:HALO_ROWS])
        shifted = jnp.concatenate([first, rolled[HALO_ROWS:]], axis=0)
        y = y + w_ref[GDN_CONV - 1 - k:GDN_CONV - k, cols] * shifted
    return _silu(y)


def _nm_kernel(*refs, chunks, rope_cols, out_cols, conv):
    n_out = len(out_cols)
    x_ref, g_ref, mod_ref, w_ref = refs[:4]
    if rope_cols:
        c_ref, sa_ref, sb_ref = refs[4:7]
    if conv:
        cw_ref, halo_scr = refs[4], refs[-1]
        refs = refs[:-1]
        conv_cols, l2_cols, q_cols, tpb = conv
        seq_start = pl.program_id(0) % tpb == 0
    out_refs, h_scr = refs[-1 - n_out:-1], refs[-1]
    h = _norm_mod(x_ref[...], g_ref[...], mod_ref, SHIFT_M, SCALE_M)
    h_scr[...] = h.astype(BF16)
    tm = h.shape[0]
    for c0, cw in chunks:
        o_idx = max(k for k, start in enumerate(out_cols) if start <= c0)
        o_ref, oc0 = out_refs[o_idx], c0 - out_cols[o_idx]
        acc = _dot(h_scr[...], w_ref[:, c0:c0 + cw])
        if conv and c0 < conv_cols:
            cols = slice(c0, c0 + cw)
            prev = jnp.where(seq_start, 0.0, halo_scr[:, cols])
            halo_scr[:, cols] = acc[tm - HALO_ROWS:tm, :]
            acc = _causal_conv_silu(acc, prev, cw_ref, cols)
            groups = []
            for g0 in range(0, cw, GDN_HEAD_DIM):
                seg = acc[:, g0:g0 + GDN_HEAD_DIM]
                if c0 + g0 < l2_cols:
                    seg = seg * lax.rsqrt(jnp.sum(seg * seg, axis=-1, keepdims=True) + EPS)
                if c0 + g0 < q_cols:
                    seg = seg * (GDN_HEAD_DIM ** -0.5)
                groups.append(seg)
            acc = jnp.concatenate(groups, axis=1)
        if c0 < rope_cols:
            reps = cw // LANES
            cos = jnp.tile(c_ref[...], (1, reps))
            sa = jnp.tile(sa_ref[...], (1, reps))
            sb = jnp.tile(sb_ref[...], (1, reps))
            half = ROPE_DIMS // 2
            acc = (acc * cos + pltpu.roll(acc, cw - half, axis=1) * sa
                   + pltpu.roll(acc, half, axis=1) * sb)
        o_ref[:, oc0:oc0 + cw] = acc.astype(o_ref.dtype)


def _norm_mod_proj(x2, g, mod, w, t, *, outs, rope=None, rope_cols=0, conv_w=None, conv_l2_cols=0,
                   conv_q_cols=0, chunk=512):
    n, d = x2.shape
    ncol = w.shape[1]
    tm = min(ROW_TILE, t)
    tpb = t // tm
    out_cols = tuple(c0 for c0, _ in outs)
    widths = [end - c0 for c0, end in zip(out_cols, out_cols[1:] + (ncol,))]
    chunks = tuple((out_cols[k] + c0, cw) for k, wd in enumerate(widths) for c0, cw in _col_chunks(wd, chunk))
    assert all(cw % LANES == 0 for _, cw in chunks)
    assert all(c0 + cw <= rope_cols or c0 >= rope_cols for c0, cw in chunks)
    in_specs = [
        pl.BlockSpec((tm, d), lambda i: (i, 0)),
        pl.BlockSpec((1, d), lambda i: (0, 0)),
        pl.BlockSpec((1, 6, d), lambda i: (i // tpb, 0, 0)),
        pl.BlockSpec((d, ncol), lambda i: (0, 0), pipeline_mode=pl.Buffered(1)),
    ]
    args = [x2, g.reshape(1, d), mod, w]
    if rope_cols:
        tab = pl.BlockSpec((tm, LANES), lambda i: (i, 0))
        in_specs += [tab, tab, tab]
        args += list(rope)
    scratch = [pltpu.VMEM((tm, d), BF16)]
    conv = None
    if conv_w is not None:
        conv_cols = conv_w.shape[1]
        assert all(c0 + cw <= conv_cols or c0 >= conv_cols for c0, cw in chunks)
        conv = (conv_cols, conv_l2_cols, conv_q_cols, tpb)
        in_specs.append(pl.BlockSpec(conv_w.shape, lambda i: (0, 0)))
        args.append(conv_w)
        scratch.append(pltpu.VMEM((HALO_ROWS, conv_cols), F32))
    return pl.pallas_call(
        functools.partial(_nm_kernel, chunks=chunks, rope_cols=rope_cols, out_cols=out_cols, conv=conv),
        out_shape=[jax.ShapeDtypeStruct((n, wd), dt) for wd, (_, dt) in zip(widths, outs)],
        grid=(n // tm,),
        in_specs=in_specs,
        out_specs=[pl.BlockSpec((tm, wd), lambda i: (i, 0)) for wd in widths],
        scratch_shapes=scratch,
        compiler_params=_params("arbitrary"),
        name="norm_mod_proj",
    )(*args)


def _pair_heads(npair):
    return [(p, a) for p in range(npair) for a in range(HEADS_PER_LANE_GROUP)]


def _pair_cols(p):
    return slice(p * LANES, (p + 1) * LANES)


def _head_rows(a):
    return slice(a * HEAD_DIM, (a + 1) * HEAD_DIM)


def _split_pair_queries(qt2):
    owner = lax.broadcasted_iota(jnp.int32, qt2.shape, 0) // HEAD_DIM
    return [jnp.where(owner == a, qt2, 0.0) for a in range(HEADS_PER_LANE_GROUP)]


def _moba_kernel(q_ref, k_ref, v_ref, o_ref, kb_scr, vt_scr, kmean_scr, sel_scr, *,
                 nblk, blk, n_sel, npair):
    i = pl.program_id(2)
    heads = _pair_heads(npair)

    @pl.when(i == 0)
    def _():
        kmean_scr[...] = jnp.zeros_like(kmean_scr)
        for p in range(npair):
            for j in range(nblk):
                rows = slice(j * blk, (j + 1) * blk)
                kj = k_ref[0, rows, _pair_cols(p)]
                kmean_scr[p, j:j + 1, :] = jnp.mean(kj, axis=0, keepdims=True)
                kb_scr[p, j] = kj.astype(BF16)
                vt_scr[p, j] = v_ref[0, rows, _pair_cols(p)].astype(F32).T.astype(BF16)

    qs = {}
    for p in range(npair):
        for a, qa in enumerate(_split_pair_queries(q_ref[0, :, _pair_cols(p)].T)):
            gate = jnp.dot(kmean_scr[p], qa, precision=HIGHEST, preferred_element_type=F32)
            row = lax.broadcasted_iota(jnp.int32, gate.shape, 0)
            cnt = jnp.zeros(gate.shape, jnp.int32)
            for jp in range(nblk):
                gj = gate[jp:jp + 1, :]
                beats = (gj > gate) | ((gj == gate) & (jp < row))
                cnt = cnt + jnp.where(beats, 1, 0) * (jp < i).astype(jnp.int32)
            sel_scr[p * HEADS_PER_LANE_GROUP + a] = jnp.where((row < i) & (cnt < n_sel), 1.0, 0.0)
            qs[p, a] = (qa * ATTN_SCALE).astype(BF16)

    def values_t(p, a, j):
        return vt_scr[p, j, _head_rows(a), :]

    s = {h: _dot(kb_scr[h[0], i], qs[h]) for h in heads}
    kidx = lax.broadcasted_iota(jnp.int32, (blk, blk), 0)
    qidx = lax.broadcasted_iota(jnp.int32, (blk, blk), 1)
    s = {h: jnp.where(kidx <= qidx, s[h], -jnp.inf) for h in heads}
    m = {h: jnp.max(s[h], axis=0, keepdims=True) for h in heads}
    p_ = {h: jnp.exp(s[h] - m[h]) for h in heads}
    l = {h: jnp.sum(p_[h], axis=0, keepdims=True) for h in heads}
    acc = {h: _dot(values_t(*h, i), p_[h].astype(BF16)) for h in heads}

    def body(j, carry):
        m, l, acc = (dict(zip(heads, c)) for c in carry)
        s = {h: _dot(kb_scr[h[0], j], qs[h]) for h in heads}
        s = {h: jnp.where(sel_scr[h[0] * HEADS_PER_LANE_GROUP + h[1], pl.ds(j, 1), :] > 0.5,
                          s[h], -jnp.inf) for h in heads}
        m_new = {h: jnp.maximum(m[h], jnp.max(s[h], axis=0, keepdims=True)) for h in heads}
        alpha = {h: jnp.exp(m[h] - m_new[h]) for h in heads}
        p_ = {h: jnp.exp(s[h] - m_new[h]) for h in heads}
        l = {h: l[h] * alpha[h] + jnp.sum(p_[h], axis=0, keepdims=True) for h in heads}
        pv = {h: _dot(values_t(*h, j), p_[h].astype(BF16)) for h in heads}
        acc = {h: acc[h] * alpha[h] + pv[h] for h in heads}
        return tuple(tuple(d[h] for h in heads) for d in (m_new, l, acc))

    carry = tuple(tuple(d[h] for h in heads) for d in (m, l, acc))
    m, l, acc = (dict(zip(heads, c)) for c in lax.fori_loop(0, i, body, carry))
    for p in range(npair):
        o_t = jnp.concatenate([acc[p, a] / l[p, a] for a in range(HEADS_PER_LANE_GROUP)], axis=0)
        o_ref[0, :, _pair_cols(p)] = o_t.T.astype(o_ref.dtype)


def _moba_attention(qk3, v3, v_col0):
    b, t, w2 = qk3.shape
    w = w2 // 2
    blk = MOBA_BLOCK
    nblk = t // blk
    hb = ATTN_HEADS_PER_STEP
    npair = hb // HEADS_PER_LANE_GROUP
    gw = hb * HEAD_DIM
    n_sel = min(MOBA_TOPK, max(nblk - 1, 1))
    gate_rows = -(-nblk // 8) * 8
    return pl.pallas_call(
        functools.partial(_moba_kernel, nblk=nblk, blk=blk, n_sel=n_sel, npair=npair),
        out_shape=jax.ShapeDtypeStruct((b, t, w), BF16),
        grid=(b, w // gw, nblk),
        in_specs=[
            pl.BlockSpec((1, blk, gw), lambda bi, hi, i: (bi, i, hi)),
            pl.BlockSpec((1, t, gw), lambda bi, hi, i: (bi, 0, w // gw + hi)),
            pl.BlockSpec((1, t, gw), lambda bi, hi, i: (bi, 0, v_col0 // gw + hi)),
        ],
        out_specs=pl.BlockSpec((1, blk, gw), lambda bi, hi, i: (bi, i, hi)),
        scratch_shapes=[
            pltpu.VMEM((npair, nblk, blk, LANES), BF16),
            pltpu.VMEM((npair, nblk, LANES, blk), BF16),
            pltpu.VMEM((npair, gate_rows, LANES), F32),
            pltpu.VMEM((hb, gate_rows, blk), F32),
        ],
        compiler_params=_params("arbitrary", "arbitrary", "arbitrary"),
        name="moba_attention",
    )(qk3, qk3, v3)


def _sb_kernel(q_ref, k_ref, v_ref, o_ref, vt_scr, *, nblk, blk, npair):
    i = pl.program_id(2)
    heads = _pair_heads(npair)

    @pl.when(i == 0)
    def _():
        for p in range(npair):
            for j in range(nblk):
                rows = slice(j * blk, (j + 1) * blk)
                vt_scr[p, j] = v_ref[0, rows, _pair_cols(p)].astype(F32).T.astype(BF16)

    qs = {}
    for p in range(npair):
        for a, qa in enumerate(_split_pair_queries(q_ref[0, :, _pair_cols(p)].astype(F32).T)):
            qs[p, a] = (qa * ATTN_SCALE).astype(BF16)
    kidx = lax.broadcasted_iota(jnp.int32, (blk, blk), 0)
    qidx = lax.broadcasted_iota(jnp.int32, (blk, blk), 1)
    causal = kidx < qidx
    later = jnp.where(qidx > kidx, 1.0, 0.0).astype(BF16)

    def block(j, tot, acc, diag):
        rows = pl.ds(pl.multiple_of(j * blk, blk), blk)
        z = {h: _dot(k_ref[0, rows, _pair_cols(h[0])], qs[h]) for h in heads}
        sp = {h: _softplus(z[h]) for h in heads}
        spm = {h: jnp.where(causal, sp[h], 0.0) for h in heads} if diag else sp
        parts = {h: _split2(spm[h]) for h in heads}
        suf = {h: _dot(later, parts[h][0]) + _dot(later, parts[h][1]) for h in heads}
        w = {h: jnp.exp(z[h] - sp[h] - suf[h] - tot[h]) for h in heads}
        if diag:
            w = {h: jnp.where(causal, w[h], 0.0) for h in heads}
        pv = {h: _dot(vt_scr[h[0], j, _head_rows(h[1]), :], w[h].astype(BF16)) for h in heads}
        acc = {h: acc[h] + pv[h] for h in heads}
        tot = {h: tot[h] + suf[h][0:1, :] + spm[h][0:1, :] for h in heads}
        return tot, acc

    tot = {h: jnp.zeros((1, blk), F32) for h in heads}
    acc = {h: jnp.zeros((HEAD_DIM, blk), F32) for h in heads}
    tot, acc = block(i, tot, acc, True)

    def pack(tot, acc):
        return tuple(tot[h] for h in heads), tuple(acc[h] for h in heads)

    def alive(carry):
        step, tot, _ = carry
        least = functools.reduce(jnp.minimum, tot)
        return (step < i) & (jnp.min(least) < SB_UNDERFLOW_LOG)

    def body(carry):
        step, tot, acc = carry
        tot, acc = block(i - 1 - step, dict(zip(heads, tot)), dict(zip(heads, acc)), False)
        return (step + 1, *pack(tot, acc))

    _, _, acc = lax.while_loop(alive, body, (jnp.int32(0), *pack(tot, acc)))
    acc = dict(zip(heads, acc))
    for p in range(npair):
        o_t = jnp.concatenate([acc[p, a] for a in range(HEADS_PER_LANE_GROUP)], axis=0)
        o_ref[0, :, _pair_cols(p)] = o_t.T.astype(o_ref.dtype)


def _sb_attention(qkv3, col0, w):
    b, t, _ = qkv3.shape
    blk = MOBA_BLOCK
    nblk = t // blk
    hb = ATTN_HEADS_PER_STEP
    npair = hb // HEADS_PER_LANE_GROUP
    gw = hb * HEAD_DIM
    first = lambda k: (col0 + k * w) // gw
    return pl.pallas_call(
        functools.partial(_sb_kernel, nblk=nblk, blk=blk, npair=npair),
        out_shape=jax.ShapeDtypeStruct((b, t, w), BF16),
        grid=(b, w // gw, nblk),
        in_specs=[
            pl.BlockSpec((1, blk, gw), lambda bi, hi, i: (bi, i, first(0) + hi)),
            pl.BlockSpec((1, t, gw), lambda bi, hi, i: (bi, 0, first(1) + hi)),
            pl.BlockSpec((1, t, gw), lambda bi, hi, i: (bi, 0, first(2) + hi)),
        ],
        out_specs=pl.BlockSpec((1, blk, gw), lambda bi, hi, i: (bi, i, hi)),
        scratch_shapes=[pltpu.VMEM((npair, nblk, LANES, blk), BF16)],
        compiler_params=_params("arbitrary", "arbitrary", "arbitrary"),
        name="sb_attention",
    )(qkv3, qkv3, qkv3)


def _layer_tail_kernel(*refs, splits, chunks, final):
    o_refs = refs[:len(splits)]
    x_ref, mod_ref, w_mix_ref, g_ref, wg_ref, wu_ref, wo_ref = refs[len(splits):len(splits) + 7]
    tail = refs[len(splits) + 7:]
    fg_ref = tail[0] if final else None
    out_ref, h_scr, a_scr = tail[-3:]
    mixed = None
    for o_ref, (r0, r1) in zip(o_refs, splits):
        part = _dot(o_ref[...], w_mix_ref[r0:r1, :])
        mixed = part if mixed is None else mixed + part
    x = x_ref[...] + mod_ref[0, GATE_M:GATE_M + 1, :] * mixed
    h_scr[...] = _norm_mod(x, g_ref[...], mod_ref, SHIFT_F, SCALE_F).astype(BF16)
    for c0, cw in chunks:
        gate = _dot(h_scr[...], wg_ref[:, c0:c0 + cw])
        up = _dot(h_scr[...], wu_ref[:, c0:c0 + cw])
        a_scr[:, c0:c0 + cw] = (_silu(gate) * up).astype(BF16)
    x = x + mod_ref[0, GATE_F:GATE_F + 1, :] * _dot(a_scr[...], wo_ref[...])
    if final:
        x = x * lax.rsqrt(jnp.mean(x * x, axis=-1, keepdims=True) + EPS) * fg_ref[...]
    out_ref[...] = x


def _layer_tail(o_parts, x2, mod, w_mix, g, wg, wu, wo, t, final_g=None):
    n, d = x2.shape
    f = wg.shape[1]
    tm = min(ROW_TILE, t)
    tpb = t // tm
    splits, r0 = [], 0
    for o in o_parts:
        splits.append((r0, r0 + o.shape[1]))
        r0 += o.shape[1]
    assert r0 == w_mix.shape[0]
    final = final_g is not None
    const = lambda shape: pl.BlockSpec(shape, lambda i: (0, 0), pipeline_mode=pl.Buffered(1))
    row = pl.BlockSpec((1, d), lambda i: (0, 0))
    in_specs = [pl.BlockSpec((tm, o.shape[1]), lambda i: (i, 0)) for o in o_parts] + [
        pl.BlockSpec((tm, d), lambda i: (i, 0)),
        pl.BlockSpec((1, 6, d), lambda i: (i // tpb, 0, 0)),
        const(w_mix.shape), row, const((d, f)), const((d, f)), const((f, d)),
    ]
    args = [*o_parts, x2, mod, w_mix, g.reshape(1, d), wg, wu, wo]
    if final:
        in_specs.append(row)
        args.append(final_g.reshape(1, d))
    return pl.pallas_call(
        functools.partial(_layer_tail_kernel, splits=tuple(splits), chunks=_col_chunks(f, 256),
                          final=final),
        out_shape=jax.ShapeDtypeStruct((n, d), F32),
        grid=(n // tm,),
        in_specs=in_specs,
        out_specs=pl.BlockSpec((tm, d), lambda i: (i, 0)),
        scratch_shapes=[pltpu.VMEM((tm, d), BF16), pltpu.VMEM((tm, f), BF16)],
        compiler_params=_params("arbitrary"),
        name="layer_tail",
    )(*args)


def _gdn_kernel(q_ref, k_ref, v_ref, z_ref, ab_ref, alog_ref, dtb_ref, ng_ref, o_ref, s_scr, *, tt):
    it = pl.program_id(1)
    ck, dk, nh = GDN_CHUNK, GDN_HEAD_DIM, GDN_HEADS
    nc = tt // ck
    chains = [(h, c) for c in range(nc) for h in range(nh)]

    @pl.when(it == 0)
    def _():
        s_scr[...] = jnp.zeros_like(s_scr)

    def head(x, h):
        return x[:, h * dk:(h + 1) * dk]

    va = v_ref[0]
    qn = [head(q_ref[0], h) for h in range(nh)]
    kn = [head(k_ref[0], h) for h in range(nh)]

    ab = ab_ref[0]
    g_all = -jnp.exp(alog_ref[...]) * _softplus(ab + dtb_ref[...])
    beta_all = jax.nn.sigmoid(ab)
    ri = lax.broadcasted_iota(jnp.int32, (tt, tt), 0)
    ci = lax.broadcasted_iota(jnp.int32, (tt, tt), 1)
    same_chunk_lower = jnp.where((ri >= ci) & (ri // ck == ci // ck), 1.0, 0.0).astype(BF16)
    g1 = g_all.astype(BF16)
    r1 = g_all - g1.astype(F32)
    g2 = r1.astype(BF16)
    g3 = (r1 - g2.astype(F32)).astype(BF16)
    gc_all = _dot(same_chunk_lower, g1) + _dot(same_chunk_lower, g2) + _dot(same_chunk_lower, g3)

    r64 = lax.broadcasted_iota(jnp.int32, (ck, ck), 0)
    c64 = lax.broadcasted_iota(jnp.int32, (ck, ck), 1)
    nt = (((1,), (1,)), ((), ()))
    tn = (((0,), (0,)), ((), ()))

    gcb, egc, kbeta, a_mat, attn, rhs = {}, {}, {}, {}, {}, {}
    raw = {}
    for h, c in chains:
        rows = slice(c * ck, (c + 1) * ck)
        gcb[h, c] = jnp.broadcast_to(gc_all[rows, h:h + 1], (ck, dk))
        betab = jnp.broadcast_to(beta_all[rows, nh + h:nh + h + 1], (ck, dk))
        egc[h, c] = jnp.exp(gcb[h, c])
        kc = kn[h][rows]
        kbeta[h, c] = kc * betab
        rhs[h, c] = jnp.concatenate([head(va, h)[rows] * betab, kbeta[h, c] * egc[h, c]], axis=1)
        lhs = jnp.concatenate([kbeta[h, c], qn[h][rows]], axis=0).astype(BF16)
        raw[h, c] = lax.dot_general(lhs, kc.astype(BF16), nt, preferred_element_type=F32)
    for h, c in chains:
        grow = gcb[h, c].T[:ck, :]
        decay = jnp.where(r64 >= c64, jnp.exp(gcb[h, c][:, :ck] - grow), 0.0)
        a_mat[h, c] = jnp.where(r64 > c64, raw[h, c][:ck] * decay, 0.0)
        attn[h, c] = (raw[h, c][ck:] * decay).astype(BF16)

    def mm3(ah, al, bh, bl):
        return _dot(ah, bh) + _dot(ah, bl) + _dot(al, bh)

    ps = {k: _split2(a_mat[k]) for k in chains}
    ys = {k: _split2(rhs[k]) for k in chains}
    y = {k: rhs[k] - mm3(*ps[k], *ys[k]) for k in chains}
    for _ in range((ck - 1).bit_length() - 1):
        pw = {k: mm3(*ps[k], *ps[k]) for k in chains}
        ps = {k: _split2(pw[k]) for k in chains}
        ys = {k: _split2(y[k]) for k in chains}
        y = {k: y[k] + mm3(*ps[k], *ys[k]) for k in chains}

    state = [s_scr[h] for h in range(nh)]
    for c in range(nc):
        rows = slice(c * ck, (c + 1) * ck)
        s16 = [state[h].astype(BF16) for h in range(nh)]
        v_new = [y[h, c][:, :dk] - _dot(y[h, c][:, dk:].astype(BF16), s16[h]) for h in range(nh)]
        vn16 = [v.astype(BF16) for v in v_new]
        o_c = [_dot((qn[h][rows] * egc[h, c]).astype(BF16), s16[h]) + _dot(attn[h, c], vn16[h])
               for h in range(nh)]
        for h in range(nh):
            g_last = gcb[h, c][ck - 1:ck, :]
            k_dec = kn[h][rows] * jnp.exp(g_last - gcb[h, c])
            state[h] = state[h] * jnp.exp(g_last) + lax.dot_general(
                k_dec.astype(BF16), vn16[h], tn, preferred_element_type=F32)
        for h in range(nh):
            o = o_c[h]
            on = o * lax.rsqrt(jnp.mean(o * o, axis=-1, keepdims=True) + EPS) * ng_ref[...]
            cols = slice(h * dk, (h + 1) * dk)
            o_ref[0, rows, cols] = (on * _silu(z_ref[0, rows, cols])).astype(o_ref.dtype)
    for h in range(nh):
        s_scr[h] = state[h]


def _gated_deltanet(proj3, a_log, dt_bias, norm_g):
    b, t, _ = proj3.shape
    nh, dk = GDN_HEADS, GDN_HEAD_DIM
    gw = nh * dk
    tt = min(GDN_ROW_TILE, t)
    col = lambda j: pl.BlockSpec((1, tt, gw), lambda bi, i: (bi, i, j))
    row = pl.BlockSpec((1, LANES), lambda bi, i: (0, 0))
    pad = lambda v: jnp.pad(v.astype(F32), (0, LANES - v.shape[0])).reshape(1, LANES)
    return pl.pallas_call(
        functools.partial(_gdn_kernel, tt=tt),
        out_shape=jax.ShapeDtypeStruct((b, t, gw), BF16),
        grid=(b, t // tt),
        in_specs=[
            col(0), col(1), col(2), col(3),
            pl.BlockSpec((1, tt, LANES), lambda bi, i: (bi, i, 4 * nh)),
            row, row, pl.BlockSpec((1, dk), lambda bi, i: (0, 0)),
        ],
        out_specs=pl.BlockSpec((1, tt, gw), lambda bi, i: (bi, i, 0)),
        scratch_shapes=[pltpu.VMEM((nh, dk, dk), F32)],
        compiler_params=_params("arbitrary", "arbitrary"),
        name="gated_deltanet",
    )(proj3, proj3, proj3, proj3, proj3, pad(a_log), pad(dt_bias), norm_g.astype(F32).reshape(1, dk))


def _attention_mixer(x2, g, mod, w_in, rope, b, t):
    mw = MOBA_HEADS * HEAD_DIM
    sw = SB_HEADS * HEAD_DIM
    qk, rest = _norm_mod_proj(x2, g, mod, w_in.astype(BF16), t, rope=rope, rope_cols=2 * mw,
                              outs=((0, F32), (2 * mw, BF16)))
    rest3 = rest.reshape(b, t, -1)
    oa = _moba_attention(qk.reshape(b, t, 2 * mw), rest3, 0)
    ob = _sb_attention(rest3, mw, sw)
    return [oa.reshape(b * t, mw), ob.reshape(b * t, sw)]


def _gdn_mixer(x2, g, mod, w_in, conv_w, a_log, dt_bias, norm_g, b, t):
    gw = GDN_HEADS * GDN_HEAD_DIM
    w_main = w_in[:, :4 * gw]
    w_ab = jnp.pad(w_in[:, 4 * gw:], ((0, 0), (0, LANES - 2 * GDN_HEADS)))
    w_cat = jnp.concatenate([w_main, w_ab], axis=1).astype(BF16)
    (proj,) = _norm_mod_proj(x2, g, mod, w_cat, t, chunk=256, outs=((0, F32),), conv_w=conv_w,
                             conv_l2_cols=2 * gw, conv_q_cols=gw)
    o = _gated_deltanet(proj.reshape(b, t, -1), a_log, dt_bias, norm_g)
    return [o.reshape(b * t, gw)]


def kernel(x, c, positions, ada_w, ada_b, norm_mix_g, norm_ffn_g, attn_w_in, attn_w_out, gdn_w_in,
           gdn_conv_w, gdn_a_log, gdn_dt_bias, gdn_norm_g, gdn_w_out, ffn_w_in, ffn_w_out,
           final_norm_g):
    b, t, d = x.shape
    depth = ada_w.shape[0]
    assert t % MOBA_BLOCK == 0 and t % GDN_CHUNK == 0
    mods = _modulation(c, ada_w, ada_b)
    rope = _rope_tables(positions)
    x2 = x.reshape(b * t, d)
    f = ffn_w_out.shape[1]
    for layer in range(depth):
        mod = mods[layer].reshape(b, 6, d)
        i = layer // 2
        if layer % 2 == 0:
            o_parts = _attention_mixer(x2, norm_mix_g[layer], mod, attn_w_in[i], rope, b, t)
            w_mix = attn_w_out[i]
        else:
            o_parts = _gdn_mixer(x2, norm_mix_g[layer], mod, gdn_w_in[i], gdn_conv_w[i], gdn_a_log[i],
                                 gdn_dt_bias[i], gdn_norm_g[i], b, t)
            w_mix = gdn_w_out[i]
        w_in = ffn_w_in[layer].astype(BF16)
        x2 = _layer_tail(o_parts, x2, mod, w_mix.astype(BF16), norm_ffn_g[layer], w_in[:, :f], w_in[:, f:],
                         ffn_w_out[layer].astype(BF16), t,
                         final_g=final_norm_g if layer == depth - 1 else None)
    return x2.reshape(b, t, d)
```

```python
import functools

import jax
import jax.numpy as jnp
from jax import lax
from jax.experimental import pallas as pl
from jax.experimental.pallas import tpu as pltpu

F32 = jnp.float32
BF16 = jnp.bfloat16
HIGHEST = lax.Precision.HIGHEST

EPS = 1e-6
HEAD_DIM = 64
MOBA_HEADS = 8
SB_HEADS = 8
MOBA_BLOCK = 256
MOBA_TOPK = 3
ROPE_THETA = 500000.0
ROPE_DIMS = HEAD_DIM // 4
GDN_HEADS = 8
GDN_HEAD_DIM = 128
GDN_CONV = 4
GDN_CHUNK = 64
ATTN_SCALE = HEAD_DIM ** -0.5

LANES = 128
HALO_ROWS = 8
V7X_VMEM_LIMIT = 56 * 1024 * 1024
ROW_TILE = 512
MOD_COL_TILE = 1536
GDN_ROW_TILE = 256
ATTN_HEADS_PER_STEP = 8
MOBA_LOOP_BLOCKS = 2
HEADS_PER_LANE_GROUP = LANES // HEAD_DIM
SB_UNDERFLOW_LOG = 110.0

SHIFT_M, SCALE_M, GATE_M, SHIFT_F, SCALE_F, GATE_F = range(6)


def _params(*sem):
    return pltpu.CompilerParams(dimension_semantics=sem, vmem_limit_bytes=V7X_VMEM_LIMIT)


def _silu(x):
    return x * jax.nn.sigmoid(x)


def _softplus(x):
    return jnp.maximum(x, 0.0) + jnp.log(1.0 + jnp.exp(-jnp.abs(x)))


def _dot(a, b):
    return jnp.dot(a, b, preferred_element_type=F32)


def _split2(x):
    hi = x.astype(BF16)
    return hi, (x - hi.astype(F32)).astype(BF16)


def _col_chunks(total, width):
    chunks, c0 = [], 0
    while c0 < total:
        cw = min(width, total - c0)
        chunks.append((c0, cw))
        c0 += cw
    return tuple(chunks)


def _mod_kernel(c_ref, w_ref, b_ref, o_ref):
    cond = _silu(c_ref[...])
    o_ref[0] = jnp.dot(cond, w_ref[0], precision=HIGHEST, preferred_element_type=F32) + b_ref[0]


def _modulation(c, ada_w, ada_b):
    depth, d, six_d = ada_w.shape
    b = c.shape[0]
    tn = MOD_COL_TILE
    return pl.pallas_call(
        _mod_kernel,
        out_shape=jax.ShapeDtypeStruct((depth, b, six_d), F32),
        grid=(depth, six_d // tn),
        in_specs=[
            pl.BlockSpec((b, d), lambda l, j: (0, 0)),
            pl.BlockSpec((1, d, tn), lambda l, j: (l, 0, j)),
            pl.BlockSpec((1, 1, tn), lambda l, j: (l, 0, j)),
        ],
        out_specs=pl.BlockSpec((1, b, tn), lambda l, j: (l, 0, j)),
        compiler_params=_params("arbitrary", "arbitrary"),
        name="adaln_mod",
    )(c, ada_w, ada_b.reshape(depth, 1, six_d))


def _rope_kernel(pos_ref, freq_ref, ma_ref, mb_ref, c_ref, sa_ref, sb_ref):
    ang = pos_ref[...] * freq_ref[...]
    s = jnp.sin(ang)
    c_ref[...] = jnp.cos(ang)
    sa_ref[...] = s * ma_ref[...]
    sb_ref[...] = s * mb_ref[...]


def _rope_tables(positions):
    n = positions.size
    half = ROPE_DIMS // 2
    inv_freq = ROPE_THETA ** (-jnp.arange(half, dtype=F32) * 2.0 / ROPE_DIMS)
    lane = jnp.arange(LANES) % HEAD_DIM
    freq = jnp.where(lane < ROPE_DIMS, inv_freq[lane % half], 0.0).astype(F32)[None, :]
    ma = jnp.where(lane < half, -1.0, 0.0).astype(F32)[None, :]
    mb = jnp.where((lane >= half) & (lane < ROPE_DIMS), 1.0, 0.0).astype(F32)[None, :]
    pos = positions.astype(F32).reshape(n, 1)
    tm = min(ROW_TILE, n)
    row = pl.BlockSpec((1, LANES), lambda i: (0, 0))
    tab = pl.BlockSpec((tm, LANES), lambda i: (i, 0))
    return pl.pallas_call(
        _rope_kernel,
        out_shape=[jax.ShapeDtypeStruct((n, LANES), F32)] * 3,
        grid=(n // tm,),
        in_specs=[pl.BlockSpec((tm, 1), lambda i: (i, 0)), row, row, row],
        out_specs=[tab, tab, tab],
        compiler_params=_params("arbitrary"),
        name="rope_tables",
    )(pos, freq, ma, mb)


def _norm_mod(x, g_row, mod_ref, shift_idx, scale_idx):
    y = x * lax.rsqrt(jnp.mean(x * x, axis=-1, keepdims=True) + EPS) * g_row
    return y * (1.0 + mod_ref[0, scale_idx:scale_idx + 1, :]) + mod_ref[0, shift_idx:shift_idx + 1, :]


def _causal_conv_silu(x, prev, w_ref, cols):
    row = lax.broadcasted_iota(jnp.int32, prev.shape, 0)
    y = w_ref[GDN_CONV - 1:GDN_CONV, cols] * x
    for k in range(1, GDN_CONV):
        rolled = pltpu.roll(x, k, axis=0)
        first = jnp.where(row < k, pltpu.roll(prev, k, axis=0), rolled[:HALO_ROWS])
        shifted = jnp.concatenate([first, rolled[HALO_ROWS:]], axis=0)
        y = y + w_ref[GDN_CONV - 1 - k:GDN_CONV - k, cols] * shifted
    return _silu(y)


def _nm_kernel(*refs, chunks, rope_cols, out_cols, conv):
    n_out = len(out_cols)
    x_ref, g_ref, mod_ref, w_ref = refs[:4]
    if rope_cols:
        c_ref, sa_ref, sb_ref = refs[4:7]
    if conv:
        cw_ref, halo_scr = refs[4], refs[-1]
        refs = refs[:-1]
        conv_cols, l2_cols, q_cols, tpb = conv
        seq_start = pl.program_id(0) % tpb == 0
    out_refs, h_scr = refs[-1 - n_out:-1], refs[-1]
    h = _norm_mod(x_ref[...], g_ref[...], mod_ref, SHIFT_M, SCALE_M)
    h_scr[...] = h.astype(BF16)
    tm = h.shape[0]
    for c0, cw in chunks:
        o_idx = max(k for k, start in enumerate(out_cols) if start <= c0)
        o_ref, oc0 = out_refs[o_idx], c0 - out_cols[o_idx]
        acc = _dot(h_scr[...], w_ref[:, c0:c0 + cw])
        if conv and c0 < conv_cols:
            cols = slice(c0, c0 + cw)
            prev = jnp.where(seq_start, 0.0, halo_scr[:, cols])
            halo_scr[:, cols] = acc[tm - HALO_ROWS:tm, :]
            acc = _causal_conv_silu(acc, prev, cw_ref, cols)
            groups = []
            for g0 in range(0, cw, GDN_HEAD_DIM):
                seg = acc[:, g0:g0 + GDN_HEAD_DIM]
                if c0 + g0 < l2_cols:
                    seg = seg * lax.rsqrt(jnp.sum(seg * seg, axis=-1, keepdims=True) + EPS)
                if c0 + g0 < q_cols:
                    seg = seg * (GDN_HEAD_DIM ** -0.5)
                groups.append(seg)
            acc = jnp.concatenate(groups, axis=1)
        if c0 < rope_cols:
            reps = cw // LANES
            cos = jnp.tile(c_ref[...], (1, reps))
            sa = jnp.tile(sa_ref[...], (1, reps))
            sb = jnp.tile(sb_ref[...], (1, reps))
            half = ROPE_DIMS // 2
            acc = (acc * cos + pltpu.roll(acc, cw - half, axis=1) * sa
                   + pltpu.roll(acc, half, axis=1) * sb)
        o_ref[:, oc0:oc0 + cw] = acc.astype(o_ref.dtype)


def _norm_mod_proj(x2, g, mod, w, t, *, outs, rope=None, rope_cols=0, conv_w=None, conv_l2_cols=0,
                   conv_q_cols=0, chunk=512):
    n, d = x2.shape
    ncol = w.shape[1]
    tm = min(ROW_TILE, t)
    tpb = t // tm
    out_cols = tuple(c0 for c0, _ in outs)
    widths = [end - c0 for c0, end in zip(out_cols, out_cols[1:] + (ncol,))]
    chunks = tuple((out_cols[k] + c0, cw) for k, wd in enumerate(widths) for c0, cw in _col_chunks(wd, chunk))
    assert all(cw % LANES == 0 for _, cw in chunks)
    assert all(c0 + cw <= rope_cols or c0 >= rope_cols for c0, cw in chunks)
    in_specs = [
        pl.BlockSpec((tm, d), lambda i: (i, 0)),
        pl.BlockSpec((1, d), lambda i: (0, 0)),
        pl.BlockSpec((1, 6, d), lambda i: (i // tpb, 0, 0)),
        pl.BlockSpec((d, ncol), lambda i: (0, 0), pipeline_mode=pl.Buffered(1)),
    ]
    args = [x2, g.reshape(1, d), mod, w]
    if rope_cols:
        tab = pl.BlockSpec((tm, LANES), lambda i: (i, 0))
        in_specs += [tab, tab, tab]
        args += list(rope)
    scratch = [pltpu.VMEM((tm, d), BF16)]
    conv = None
    if conv_w is not None:
        conv_cols = conv_w.shape[1]
        assert all(c0 + cw <= conv_cols or c0 >= conv_cols for c0, cw in chunks)
        conv = (conv_cols, conv_l2_cols, conv_q_cols, tpb)
        in_specs.append(pl.BlockSpec(conv_w.shape, lambda i: (0, 0)))
        args.append(conv_w)
        scratch.append(pltpu.VMEM((HALO_ROWS, conv_cols), F32))
    return pl.pallas_call(
        functools.partial(_nm_kernel, chunks=chunks, rope_cols=rope_cols, out_cols=out_cols, conv=conv),
        out_shape=[jax.ShapeDtypeStruct((n, wd), dt) for wd, (_, dt) in zip(widths, outs)],
        grid=(n // tm,),
        in_specs=in_specs,
        out_specs=[pl.BlockSpec((tm, wd), lambda i: (i, 0)) for wd in widths],
        scratch_shapes=scratch,
        compiler_params=_params("arbitrary"),
        name="norm_mod_proj",
    )(*args)


def _pair_heads(npair):
    return [(p, a) for p in range(npair) for a in range(HEADS_PER_LANE_GROUP)]


def _pair_cols(p):
    return slice(p * LANES, (p + 1) * LANES)


def _head_rows(a):
    return slice(a * HEAD_DIM, (a + 1) * HEAD_DIM)


def _split_pair_queries(qt2):
    owner = lax.broadcasted_iota(jnp.int32, qt2.shape, 0) // HEAD_DIM
    return [jnp.where(owner == a, qt2, 0.0) for a in range(HEADS_PER_LANE_GROUP)]


def _moba_kernel(q_ref, k_ref, v_ref, o_ref, kb_scr, vt_scr, kmean_scr, sel_scr, *,
                 nblk, blk, n_sel, npair):
    i = pl.program_id(2)
    heads = _pair_heads(npair)

    @pl.when(i == 0)
    def _():
        kmean_scr[...] = jnp.zeros_like(kmean_scr)
        for p in range(npair):
            for j in range(nblk):
                rows = slice(j * blk, (j + 1) * blk)
                kj = k_ref[0, rows, _pair_cols(p)]
                kmean_scr[p, j:j + 1, :] = jnp.mean(kj, axis=0, keepdims=True)
                kb_scr[p, j] = kj.astype(BF16)
                vt_scr[p, j] = v_ref[0, rows, _pair_cols(p)].astype(F32).T.astype(BF16)

    qs = {}
    for p in range(npair):
        for a, qa in enumerate(_split_pair_queries(q_ref[0, :, _pair_cols(p)].T)):
            gate = jnp.dot(kmean_scr[p], qa, precision=HIGHEST, preferred_element_type=F32)
            row = lax.broadcasted_iota(jnp.int32, gate.shape, 0)
            cnt = jnp.zeros(gate.shape, jnp.int32)
            for jp in range(nblk):
                gj = gate[jp:jp + 1, :]
                beats = (gj > gate) | ((gj == gate) & (jp < row))
                cnt = cnt + jnp.where(beats, 1, 0) * (jp < i).astype(jnp.int32)
            sel_scr[p * HEADS_PER_LANE_GROUP + a] = jnp.where((row < i) & (cnt < n_sel), 1.0, 0.0)
            qs[p, a] = (qa * ATTN_SCALE).astype(BF16)

    def values_t(p, a, j):
        return vt_scr[p, j, _head_rows(a), :]

    s = {h: _dot(kb_scr[h[0], i], qs[h]) for h in heads}
    kidx = lax.broadcasted_iota(jnp.int32, (blk, blk), 0)
    qidx = lax.broadcasted_iota(jnp.int32, (blk, blk), 1)
    s = {h: jnp.where(kidx <= qidx, s[h], -jnp.inf) for h in heads}
    m = {h: jnp.max(s[h], axis=0, keepdims=True) for h in heads}
    p_ = {h: jnp.exp(s[h] - m[h]) for h in heads}
    l = {h: jnp.sum(p_[h], axis=0, keepdims=True) for h in heads}
    acc = {h: _dot(values_t(*h, i), p_[h].astype(BF16)) for h in heads}

    def body(it, carry):
        m, l, acc = (dict(zip(heads, c)) for c in carry)
        js = [it * MOBA_LOOP_BLOCKS + u for u in range(MOBA_LOOP_BLOCKS)]
        hj = [(h, u) for u in range(MOBA_LOOP_BLOCKS) for h in heads]
        s = {(h, u): _dot(kb_scr[h[0], js[u]], qs[h]) for h, u in hj}
        s = {(h, u): jnp.where(sel_scr[h[0] * HEADS_PER_LANE_GROUP + h[1], pl.ds(js[u], 1), :] > 0.5,
                               s[h, u], -jnp.inf) for h, u in hj}
        m_new = {h: functools.reduce(jnp.maximum, [m[h]] + [jnp.max(s[h, u], axis=0, keepdims=True)
                                                            for u in range(MOBA_LOOP_BLOCKS)])
                 for h in heads}
        alpha = {h: jnp.exp(m[h] - m_new[h]) for h in heads}
        p_ = {(h, u): jnp.exp(s[h, u] - m_new[h]) for h, u in hj}
        psum = {h: sum(jnp.sum(p_[h, u], axis=0, keepdims=True) for u in range(MOBA_LOOP_BLOCKS))
                for h in heads}
        l = {h: l[h] * alpha[h] + psum[h] for h in heads}
        pv = {(h, u): _dot(values_t(*h, js[u]), p_[h, u].astype(BF16)) for h, u in hj}
        acc = {h: acc[h] * alpha[h] + sum(pv[h, u] for u in range(MOBA_LOOP_BLOCKS)) for h in heads}
        return tuple(tuple(d[h] for h in heads) for d in (m_new, l, acc))

    carry = tuple(tuple(d[h] for h in heads) for d in (m, l, acc))
    trips = (i + MOBA_LOOP_BLOCKS - 1) // MOBA_LOOP_BLOCKS
    m, l, acc = (dict(zip(heads, c)) for c in lax.fori_loop(0, trips, body, carry))
    for p in range(npair):
        o_t = jnp.concatenate([acc[p, a] / l[p, a] for a in range(HEADS_PER_LANE_GROUP)], axis=0)
        o_ref[0, :, _pair_cols(p)] = o_t.T.astype(o_ref.dtype)


def _moba_attention(qk3, v3, v_col0):
    b, t, w2 = qk3.shape
    w = w2 // 2
    blk = MOBA_BLOCK
    nblk = t // blk
    hb = ATTN_HEADS_PER_STEP
    npair = hb // HEADS_PER_LANE_GROUP
    gw = hb * HEAD_DIM
    n_sel = min(MOBA_TOPK, max(nblk - 1, 1))
    gate_rows = -(-nblk // 8) * 8
    return pl.pallas_call(
        functools.partial(_moba_kernel, nblk=nblk, blk=blk, n_sel=n_sel, npair=npair),
        out_shape=jax.ShapeDtypeStruct((b, t, w), BF16),
        grid=(b, w // gw, nblk),
        in_specs=[
            pl.BlockSpec((1, blk, gw), lambda bi, hi, i: (bi, i, hi)),
            pl.BlockSpec((1, t, gw), lambda bi, hi, i: (bi, 0, w // gw + hi)),
            pl.BlockSpec((1, t, gw), lambda bi, hi, i: (bi, 0, v_col0 // gw + hi)),
        ],
        out_specs=pl.BlockSpec((1, blk, gw), lambda bi, hi, i: (bi, i, hi)),
        scratch_shapes=[
            pltpu.VMEM((npair, nblk, blk, LANES), BF16),
            pltpu.VMEM((npair, nblk, LANES, blk), BF16),
            pltpu.VMEM((npair, gate_rows, LANES), F32),
            pltpu.VMEM((hb, gate_rows, blk), F32),
        ],
        compiler_params=_params("arbitrary", "arbitrary", "arbitrary"),
        name="moba_attention",
    )(qk3, qk3, v3)


def _sb_kernel(q_ref, k_ref, v_ref, o_ref, vt_scr, *, nblk, blk, npair):
    i = pl.program_id(2)
    heads = _pair_heads(npair)

    @pl.when(i == 0)
    def _():
        for p in range(npair):
            for j in range(nblk):
                rows = slice(j * blk, (j + 1) * blk)
                vt_scr[p, j] = v_ref[0, rows, _pair_cols(p)].astype(F32).T.astype(BF16)

    qs = {}
    for p in range(npair):
        for a, qa in enumerate(_split_pair_queries(q_ref[0, :, _pair_cols(p)].astype(F32).T)):
            qs[p, a] = (qa * ATTN_SCALE).astype(BF16)
    kidx = lax.broadcasted_iota(jnp.int32, (blk, blk), 0)
    qidx = lax.broadcasted_iota(jnp.int32, (blk, blk), 1)
    causal = kidx < qidx
    later = jnp.where(qidx > kidx, 1.0, 0.0).astype(BF16)

    def block(j, tot, acc, diag):
        rows = pl.ds(pl.multiple_of(j * blk, blk), blk)
        z = {h: _dot(k_ref[0, rows, _pair_cols(h[0])], qs[h]) for h in heads}
        sp = {h: _softplus(z[h]) for h in heads}
        spm = {h: jnp.where(causal, sp[h], 0.0) for h in heads} if diag else sp
        parts = {h: _split2(spm[h]) for h in heads}
        suf = {h: _dot(later, parts[h][0]) + _dot(later, parts[h][1]) for h in heads}
        w = {h: jnp.exp(z[h] - sp[h] - suf[h] - tot[h]) for h in heads}
        if diag:
            w = {h: jnp.where(causal, w[h], 0.0) for h in heads}
        pv = {h: _dot(vt_scr[h[0], j, _head_rows(h[1]), :], w[h].astype(BF16)) for h in heads}
        acc = {h: acc[h] + pv[h] for h in heads}
        tot = {h: tot[h] + suf[h][0:1, :] + spm[h][0:1, :] for h in heads}
        return tot, acc

    tot = {h: jnp.zeros((1, blk), F32) for h in heads}
    acc = {h: jnp.zeros((HEAD_DIM, blk), F32) for h in heads}
    tot, acc = block(i, tot, acc, True)

    def pack(tot, acc):
        return tuple(tot[h] for h in heads), tuple(acc[h] for h in heads)

    def alive(carry):
        step, tot, _ = carry
        least = functools.reduce(jnp.minimum, tot)
        return (step < i) & (jnp.min(least) < SB_UNDERFLOW_LOG)

    def body(carry):
        step, tot, acc = carry
        tot, acc = block(i - 1 - step, dict(zip(heads, tot)), dict(zip(heads, acc)), False)
        return (step + 1, *pack(tot, acc))

    _, _, acc = lax.while_loop(alive, body, (jnp.int32(0), *pack(tot, acc)))
    acc = dict(zip(heads, acc))
    for p in range(npair):
        o_t = jnp.concatenate([acc[p, a] for a in range(HEADS_PER_LANE_GROUP)], axis=0)
        o_ref[0, :, _pair_cols(p)] = o_t.T.astype(o_ref.dtype)


def _sb_attention(qkv3, col0, w):
    b, t, _ = qkv3.shape
    blk = MOBA_BLOCK
    nblk = t // blk
    hb = ATTN_HEADS_PER_STEP
    npair = hb // HEADS_PER_LANE_GROUP
    gw = hb * HEAD_DIM
    first = lambda k: (col0 + k * w) // gw
    return pl.pallas_call(
        functools.partial(_sb_kernel, nblk=nblk, blk=blk, npair=npair),
        out_shape=jax.ShapeDtypeStruct((b, t, w), BF16),
        grid=(b, w // gw, nblk),
        in_specs=[
            pl.BlockSpec((1, blk, gw), lambda bi, hi, i: (bi, i, first(0) + hi)),
            pl.BlockSpec((1, t, gw), lambda bi, hi, i: (bi, 0, first(1) + hi)),
            pl.BlockSpec((1, t, gw), lambda bi, hi, i: (bi, 0, first(2) + hi)),
        ],
        out_specs=pl.BlockSpec((1, blk, gw), lambda bi, hi, i: (bi, i, hi)),
        scratch_shapes=[pltpu.VMEM((npair, nblk, LANES, blk), BF16)],
        compiler_params=_params("arbitrary", "arbitrary", "arbitrary"),
        name="sb_attention",
    )(qkv3, qkv3, qkv3)


def _layer_tail_kernel(*refs, splits, chunks, final):
    o_refs = refs[:len(splits)]
    x_ref, mod_ref, w_mix_ref, g_ref, wg_ref, wu_ref, wo_ref = refs[len(splits):len(splits) + 7]
    tail = refs[len(splits) + 7:]
    fg_ref = tail[0] if final else None
    out_ref, h_scr, a_scr = tail[-3:]
    mixed = None
    for o_ref, (r0, r1) in zip(o_refs, splits):
        part = _dot(o_ref[...], w_mix_ref[r0:r1, :])
        mixed = part if mixed is None else mixed + part
    x = x_ref[...] + mod_ref[0, GATE_M:GATE_M + 1, :] * mixed
    h_scr[...] = _norm_mod(x, g_ref[...], mod_ref, SHIFT_F, SCALE_F).astype(BF16)
    for c0, cw in chunks:
        gate = _dot(h_scr[...], wg_ref[:, c0:c0 + cw])
        up = _dot(h_scr[...], wu_ref[:, c0:c0 + cw])
        a_scr[:, c0:c0 + cw] = (_silu(gate) * up).astype(BF16)
    x = x + mod_ref[0, GATE_F:GATE_F + 1, :] * _dot(a_scr[...], wo_ref[...])
    if final:
        x = x * lax.rsqrt(jnp.mean(x * x, axis=-1, keepdims=True) + EPS) * fg_ref[...]
    out_ref[...] = x


def _layer_tail(o_parts, x2, mod, w_mix, g, wg, wu, wo, t, final_g=None):
    n, d = x2.shape
    f = wg.shape[1]
    tm = min(ROW_TILE, t)
    tpb = t // tm
    splits, r0 = [], 0
    for o in o_parts:
        splits.append((r0, r0 + o.shape[1]))
        r0 += o.shape[1]
    assert r0 == w_mix.shape[0]
    final = final_g is not None
    const = lambda shape: pl.BlockSpec(shape, lambda i: (0, 0), pipeline_mode=pl.Buffered(1))
    row = pl.BlockSpec((1, d), lambda i: (0, 0))
    in_specs = [pl.BlockSpec((tm, o.shape[1]), lambda i: (i, 0)) for o in o_parts] + [
        pl.BlockSpec((tm, d), lambda i: (i, 0)),
        pl.BlockSpec((1, 6, d), lambda i: (i // tpb, 0, 0)),
        const(w_mix.shape), row, const((d, f)), const((d, f)), const((f, d)),
    ]
    args = [*o_parts, x2, mod, w_mix, g.reshape(1, d), wg, wu, wo]
    if final:
        in_specs.append(row)
        args.append(final_g.reshape(1, d))
    return pl.pallas_call(
        functools.partial(_layer_tail_kernel, splits=tuple(splits), chunks=_col_chunks(f, 256),
                          final=final),
        out_shape=jax.ShapeDtypeStruct((n, d), F32),
        grid=(n // tm,),
        in_specs=in_specs,
        out_specs=pl.BlockSpec((tm, d), lambda i: (i, 0)),
        scratch_shapes=[pltpu.VMEM((tm, d), BF16), pltpu.VMEM((tm, f), BF16)],
        compiler_params=_params("arbitrary"),
        name="layer_tail",
    )(*args)


def _gdn_kernel(q_ref, k_ref, v_ref, z_ref, ab_ref, alog_ref, dtb_ref, ng_ref, o_ref, s_scr, *, tt):
    it = pl.program_id(1)
    ck, dk, nh = GDN_CHUNK, GDN_HEAD_DIM, GDN_HEADS
    nc = tt // ck
    chains = [(h, c) for c in range(nc) for h in range(nh)]

    @pl.when(it == 0)
    def _():
        s_scr[...] = jnp.zeros_like(s_scr)

    def head(x, h):
        return x[:, h * dk:(h + 1) * dk]

    va = v_ref[0]
    qn = [head(q_ref[0], h) for h in range(nh)]
    kn = [head(k_ref[0], h) for h in range(nh)]

    ab = ab_ref[0]
    g_all = -jnp.exp(alog_ref[...]) * _softplus(ab + dtb_ref[...])
    beta_all = jax.nn.sigmoid(ab)
    ri = lax.broadcasted_iota(jnp.int32, (tt, tt), 0)
    ci = lax.broadcasted_iota(jnp.int32, (tt, tt), 1)
    same_chunk_lower = jnp.where((ri >= ci) & (ri // ck == ci // ck), 1.0, 0.0).astype(BF16)
    g1 = g_all.astype(BF16)
    r1 = g_all - g1.astype(F32)
    g2 = r1.astype(BF16)
    g3 = (r1 - g2.astype(F32)).astype(BF16)
    gc_all = _dot(same_chunk_lower, g1) + _dot(same_chunk_lower, g2) + _dot(same_chunk_lower, g3)

    r64 = lax.broadcasted_iota(jnp.int32, (ck, ck), 0)
    c64 = lax.broadcasted_iota(jnp.int32, (ck, ck), 1)
    nt = (((1,), (1,)), ((), ()))
    tn = (((0,), (0,)), ((), ()))

    gcb, egc, kbeta, a_mat, attn, rhs = {}, {}, {}, {}, {}, {}
    raw = {}
    for h, c in chains:
        rows = slice(c * ck, (c + 1) * ck)
        gcb[h, c] = jnp.broadcast_to(gc_all[rows, h:h + 1], (ck, dk))
        betab = jnp.broadcast_to(beta_all[rows, nh + h:nh + h + 1], (ck, dk))
        egc[h, c] = jnp.exp(gcb[h, c])
        kc = kn[h][rows]
        kbeta[h, c] = kc * betab
        rhs[h, c] = jnp.concatenate([head(va, h)[rows] * betab, kbeta[h, c] * egc[h, c]], axis=1)
        lhs = jnp.concatenate([kbeta[h, c], qn[h][rows]], axis=0).astype(BF16)
        raw[h, c] = lax.dot_general(lhs, kc.astype(BF16), nt, preferred_element_type=F32)
    for h, c in chains:
        grow = gcb[h, c].T[:ck, :]
        decay = jnp.where(r64 >= c64, jnp.exp(gcb[h, c][:, :ck] - grow), 0.0)
        a_mat[h, c] = jnp.where(r64 > c64, raw[h, c][:ck] * decay, 0.0)
        attn[h, c] = (raw[h, c][ck:] * decay).astype(BF16)

    def mm3(ah, al, bh, bl):
        return _dot(ah, bh) + _dot(ah, bl) + _dot(al, bh)

    ps = {k: _split2(a_mat[k]) for k in chains}
    ys = {k: _split2(rhs[k]) for k in chains}
    y = {k: rhs[k] - mm3(*ps[k], *ys[k]) for k in chains}
    for _ in range((ck - 1).bit_length() - 1):
        pw = {k: mm3(*ps[k], *ps[k]) for k in chains}
        ps = {k: _split2(pw[k]) for k in chains}
        ys = {k: _split2(y[k]) for k in chains}
        y = {k: y[k] + mm3(*ps[k], *ys[k]) for k in chains}

    state = [s_scr[h] for h in range(nh)]
    for c in range(nc):
        rows = slice(c * ck, (c + 1) * ck)
        s16 = [state[h].astype(BF16) for h in range(nh)]
        v_new = [y[h, c][:, :dk] - _dot(y[h, c][:, dk:].astype(BF16), s16[h]) for h in range(nh)]
        vn16 = [v.astype(BF16) for v in v_new]
        o_c = [_dot((qn[h][rows] * egc[h, c]).astype(BF16), s16[h]) + _dot(attn[h, c], vn16[h])
               for h in range(nh)]
        for h in range(nh):
            g_last = gcb[h, c][ck - 1:ck, :]
            k_dec = kn[h][rows] * jnp.exp(g_last - gcb[h, c])
            state[h] = state[h] * jnp.exp(g_last) + lax.dot_general(
                k_dec.astype(BF16), vn16[h], tn, preferred_element_type=F32)
        for h in range(nh):
            o = o_c[h]
            on = o * lax.rsqrt(jnp.mean(o * o, axis=-1, keepdims=True) + EPS) * ng_ref[...]
            cols = slice(h * dk, (h + 1) * dk)
            o_ref[0, rows, cols] = (on * _silu(z_ref[0, rows, cols])).astype(o_ref.dtype)
    for h in range(nh):
        s_scr[h] = state[h]


def _gated_deltanet(proj3, a_log, dt_bias, norm_g):
    b, t, _ = proj3.shape
    nh, dk = GDN_HEADS, GDN_HEAD_DIM
    gw = nh * dk
    tt = min(GDN_ROW_TILE, t)
    col = lambda j: pl.BlockSpec((1, tt, gw), lambda bi, i: (bi, i, j))
    row = pl.BlockSpec((1, LANES), lambda bi, i: (0, 0))
    pad = lambda v: jnp.pad(v.astype(F32), (0, LANES - v.shape[0])).reshape(1, LANES)
    return pl.pallas_call(
        functools.partial(_gdn_kernel, tt=tt),
        out_shape=jax.ShapeDtypeStruct((b, t, gw), BF16),
        grid=(b, t // tt),
        in_specs=[
            col(0), col(1), col(2), col(3),
            pl.BlockSpec((1, tt, LANES), lambda bi, i: (bi, i, 4 * nh)),
            row, row, pl.BlockSpec((1, dk), lambda bi, i: (0, 0)),
        ],
        out_specs=pl.BlockSpec((1, tt, gw), lambda bi, i: (bi, i, 0)),
        scratch_shapes=[pltpu.VMEM((nh, dk, dk), F32)],
        compiler_params=_params("arbitrary", "arbitrary"),
        name="gated_deltanet",
    )(proj3, proj3, proj3, proj3, proj3, pad(a_log), pad(dt_bias), norm_g.astype(F32).reshape(1, dk))


def _attention_mixer(x2, g, mod, w_in, rope, b, t):
    mw = MOBA_HEADS * HEAD_DIM
    sw = SB_HEADS * HEAD_DIM
    qk, rest = _norm_mod_proj(x2, g, mod, w_in.astype(BF16), t, rope=rope, rope_cols=2 * mw,
                              outs=((0, F32), (2 * mw, BF16)))
    rest3 = rest.reshape(b, t, -1)
    oa = _moba_attention(qk.reshape(b, t, 2 * mw), rest3, 0)
    ob = _sb_attention(rest3, mw, sw)
    return [oa.reshape(b * t, mw), ob.reshape(b * t, sw)]


def _gdn_mixer(x2, g, mod, w_in, conv_w, a_log, dt_bias, norm_g, b, t):
    gw = GDN_HEADS * GDN_HEAD_DIM
    w_main = w_in[:, :4 * gw]
    w_ab = jnp.pad(w_in[:, 4 * gw:], ((0, 0), (0, LANES - 2 * GDN_HEADS)))
    w_cat = jnp.concatenate([w_main, w_ab], axis=1).astype(BF16)
    (proj,) = _norm_mod_proj(x2, g, mod, w_cat, t, chunk=256, outs=((0, F32),), conv_w=conv_w,
                             conv_l2_cols=2 * gw, conv_q_cols=gw)
    o = _gated_deltanet(proj.reshape(b, t, -1), a_log, dt_bias, norm_g)
    return [o.reshape(b * t, gw)]


def kernel(x, c, positions, ada_w, ada_b, norm_mix_g, norm_ffn_g, attn_w_in, attn_w_out, gdn_w_in,
           gdn_conv_w, gdn_a_log, gdn_dt_bias, gdn_norm_g, gdn_w_out, ffn_w_in, ffn_w_out,
           final_norm_g):
    b, t, d = x.shape
    depth = ada_w.shape[0]
    assert t % MOBA_BLOCK == 0 and t % GDN_CHUNK == 0
    mods = _modulation(c, ada_w, ada_b)
    rope = _rope_tables(positions)
    x2 = x.reshape(b * t, d)
    f = ffn_w_out.shape[1]
    for layer in range(depth):
        mod = mods[layer].reshape(b, 6, d)
        i = layer // 2
        if layer % 2 == 0:
            o_parts = _attention_mixer(x2, norm_mix_g[layer], mod, attn_w_in[i], rope, b, t)
            w_mix = attn_w_out[i]
        else:
            o_parts = _gdn_mixer(x2, norm_mix_g[layer], mod, gdn_w_in[i], gdn_conv_w[i], gdn_a_log[i],
                                 gdn_dt_bias[i], gdn_norm_g[i], b, t)
            w_mix = gdn_w_out[i]
        w_in = ffn_w_in[layer].astype(BF16)
        x2 = _layer_tail(o_parts, x2, mod, w_mix.astype(BF16), norm_ffn_g[layer], w_in[:, :f], w_in[:, f:],
                         ffn_w_out[layer].astype(BF16), t,
                         final_g=final_norm_g if layer == depth - 1 else None)
    return x2.reshape(b, t, d)
```
